```python
import math
import jax
import jax.numpy as jnp
from jax import lax
import numpy as np

D_MODEL = 1024
BATCH = 2
SEQ = 8192
DEPTH = 4
DEC_BATCH = 128
DEC_SEQ = 4
PAST_LEN = 2048
PAGE_SIZE = 128

N_A_LAYERS = DEPTH // 2
N_B_LAYERS = DEPTH - N_A_LAYERS
H_A = 8
DK_A = D_MODEL // H_A
DV_A = D_MODEL // H_A
QK_DIM_A = H_A * DK_A
V_DIM_A = H_A * DV_A
QKV_DIM = 2 * QK_DIM_A + V_DIM_A
IN_DIM_A = QKV_DIM + V_DIM_A + 2 * H_A
CONV_W = 4
DELTA_CHUNK = 64
H_B = 8
D_HEAD_B = D_MODEL // (2 * H_B)
K_DIM_B = H_B * 2 * D_HEAD_B
V_DIM_B = H_B * 2 * D_HEAD_B
ROPE_THETA = 10000.0
Q_BLOCK = 128
F_DENSE = 2816
N_EXPERTS = 8
TOP_K = 2
F_EXPERT = 3584
MOE_BLOCK = 128
N_DENSE = (DEPTH + 1) // 2
N_MOE = DEPTH // 2
DEEPNORM_ALPHA = (2.0 * DEPTH) ** 0.25
DEEPNORM_BETA = (8.0 * DEPTH) ** -0.25
LN_EPS = 1e-5
RMS_EPS = 1e-6

kernel_name = 'yoco_gated_deltanet_diff_attention_step'


def layer_norm(x, g, b):
    xf = x.astype(jnp.float32)
    mu = jnp.mean(xf, -1, keepdims=True)
    var = jnp.mean(jnp.square(xf - mu), -1, keepdims=True)
    return ((xf - mu) * lax.rsqrt(var + LN_EPS) * g + b).astype(x.dtype)


def rms_norm(x, g):
    xf = x.astype(jnp.float32)
    return (xf * lax.rsqrt(jnp.mean(jnp.square(xf), -1, keepdims=True) + RMS_EPS) * g).astype(x.dtype)


def l2_normalize(x):
    xf = x.astype(jnp.float32)
    return (xf * lax.rsqrt(jnp.sum(jnp.square(xf), -1, keepdims=True) + 1e-6)).astype(x.dtype)


def rope(x, pos):
    half = x.shape[-1] // 2
    inv_freq = ROPE_THETA ** (-jnp.arange(half, dtype=jnp.float32) / half)
    ang = pos.astype(jnp.float32)[:, None] * inv_freq[None, :]
    cos = jnp.cos(ang)[None, :, None, None, :]
    sin = jnp.sin(ang)[None, :, None, None, :]
    xf = x.astype(jnp.float32)
    x1, x2 = xf[..., :half], xf[..., half:]
    return jnp.concatenate([x1 * cos - x2 * sin, x2 * cos + x1 * sin], -1).astype(x.dtype)


def short_conv_silu(x, buf, w):
    T = x.shape[1]
    xp = jnp.concatenate([buf.astype(x.dtype), x], axis=1)
    y = sum(w[i] * xp[:, i:i + T] for i in range(CONV_W))
    return jax.nn.silu(y), xp[:, T:]


def gated_delta_rule(q, k, v, g, beta, s0):
    in_dtype = v.dtype
    f32 = jnp.float32
    Bsz, T, H, DK = q.shape
    DV = v.shape[-1]
    C = min(DELTA_CHUNK, T)
    pad = (-T) % C
    q, k, v, g, beta = (t.astype(f32) for t in (q, k, v, g, beta))
    if pad:
        padw = lambda t: jnp.pad(t, [(0, 0), (0, pad)] + [(0, 0)] * (t.ndim - 2))
        q, k, v, g, beta = (padw(t) for t in (q, k, v, g, beta))
    N = (T + pad) // C

    def blk(t):
        t = t.reshape((Bsz, N, C, H) + t.shape[3:])
        return jnp.moveaxis(t, 3, 1)

    q, k, v, g, beta = (blk(t) for t in (q, k, v, g, beta))
    G = jnp.cumsum(g, axis=-1)
    causal = jnp.tril(jnp.ones((C, C), bool))
    strict = jnp.tril(jnp.ones((C, C), bool), -1)
    decay = jnp.where(causal, jnp.exp(jnp.where(causal, G[..., :, None] - G[..., None, :], 0.0)), 0.0)
    kb = k * beta[..., None]
    l_mat = jnp.where(strict, jnp.einsum('bhnid,bhnjd->bhnij', kb, k) * decay, 0.0)
    rhs = jnp.concatenate([v * beta[..., None], kb * jnp.exp(G)[..., None]], -1)
    sol = lax.linalg.triangular_solve(l_mat + jnp.eye(C, dtype=f32), rhs, left_side=True, lower=True)
    u_val, w_dec = sol[..., :DV], sol[..., DV:]
    attn = jnp.einsum('bhnid,bhnjd->bhnij', q, k) * decay
    q_dec = q * jnp.exp(G)[..., None]
    g_last = G[..., -1]
    k_dec = k * jnp.exp(g_last[..., None] - G)[..., None]
    xs = tuple(jnp.moveaxis(t, 2, 0) for t in (u_val, w_dec, attn, q_dec, k_dec, g_last))

    def step(S, xs_c):
        u_c, w_c, a_c, qd_c, kd_c, gl_c = xs_c
        v_new = u_c - jnp.einsum('bhid,bhde->bhie', w_c, S)
        o_c = jnp.einsum('bhid,bhde->bhie', qd_c, S) + jnp.einsum('bhij,bhje->bhie', a_c, v_new)
        S = S * jnp.exp(gl_c)[..., None, None] + jnp.einsum('bhid,bhie->bhde', kd_c, v_new)
        return S, o_c

    s_fin, o = lax.scan(step, s0.astype(f32), xs)
    o = o.transpose(1, 0, 3, 2, 4).reshape(Bsz, N * C, H, DV)[:, :T]
    return o.astype(in_dtype), s_fin.astype(s0.dtype)


def gated_deltanet(x, conv_buf, s0, w_in, w_conv, a_log, dt_bias, g_norm, w_out):
    Bsz, T, _ = x.shape
    proj = x @ w_in
    qkv, z, a, b = jnp.split(proj, [QKV_DIM, QKV_DIM + V_DIM_A, QKV_DIM + V_DIM_A + H_A], axis=-1)
    qkv, new_buf = short_conv_silu(qkv, conv_buf, w_conv)
    q, k, v = jnp.split(qkv, [QK_DIM_A, 2 * QK_DIM_A], axis=-1)
    q = l2_normalize(q.reshape(Bsz, T, H_A, DK_A)) * (DK_A ** -0.5)
    k = l2_normalize(k.reshape(Bsz, T, H_A, DK_A))
    v = v.reshape(Bsz, T, H_A, DV_A)
    beta = jax.nn.sigmoid(b.astype(jnp.float32))
    g = -jnp.exp(a_log.astype(jnp.float32)) * jax.nn.softplus(a.astype(jnp.float32) + dt_bias)
    o, s_new = gated_delta_rule(q, k, v, g, beta, s0)
    o = rms_norm(o, g_norm) * jax.nn.silu(z.reshape(Bsz, T, H_A, DV_A))
    return o.reshape(Bsz, T, V_DIM_A) @ w_out, new_buf, s_new


def shared_kv(h, pos, w_kv):
    Bsz, T, _ = h.shape
    kv = h @ w_kv
    k = rope(kv[..., :K_DIM_B].reshape(Bsz, T, H_B, 2, D_HEAD_B), pos)
    v = kv[..., K_DIM_B:].reshape(Bsz, T, H_B, 2 * D_HEAD_B)
    return k, v


def prompt_diff_attention(q, k, v, lam):
    Bsz, T = q.shape[:2]
    qb_len = min(Q_BLOCK, T)
    n_blk = T // qb_len
    k_pos = jnp.arange(T)

    def block(i):
        qb = lax.dynamic_slice_in_dim(q, i * qb_len, qb_len, axis=1)
        s = jnp.einsum('bqhmd,bkhmd->bhmqk', qb, k, preferred_element_type=jnp.float32)
        q_pos = i * qb_len + jnp.arange(qb_len)
        mask = k_pos[None, :] <= q_pos[:, None]
        p = jax.nn.softmax(jnp.where(mask, s, -jnp.inf), axis=-1)
        a = (p[:, :, 0] - lam * p[:, :, 1]).astype(v.dtype)
        return jnp.einsum('bhqk,bkhe->bqhe', a, v)

    o = lax.map(block, jnp.arange(n_blk))
    return o.transpose(1, 0, 2, 3, 4).reshape(Bsz, T, H_B, 2 * D_HEAD_B)


def sample_diff_attention(q, k_past, v_past, k_new, v_new, lam):
    T = q.shape[1]
    P = k_past.shape[1]
    s_past = jnp.einsum('bqhmd,bkhmd->bhmqk', q, k_past, preferred_element_type=jnp.float32)
    s_new = jnp.einsum('bqhmd,bkhmd->bhmqk', q, k_new, preferred_element_type=jnp.float32)
    s_new = jnp.where(jnp.tril(jnp.ones((T, T), bool)), s_new, -jnp.inf)
    p = jax.nn.softmax(jnp.concatenate([s_past, s_new], -1), axis=-1)
    a = (p[:, :, 0] - lam * p[:, :, 1]).astype(v_new.dtype)
    return (jnp.einsum('bhqk,bkhe->bqhe', a[..., :P], v_past)
            + jnp.einsum('bhqk,bkhe->bqhe', a[..., P:], v_new))


def diff_attention(h, pos, k_sh, v_sh, k_past, v_past, w_q, lam_p, g_sub, w_o, lam_init):
    Bsz, T, _ = h.shape
    q = rope((h @ w_q).reshape(Bsz, T, H_B, 2, D_HEAD_B), pos) * (D_HEAD_B ** -0.5)
    lf = lam_p.astype(jnp.float32)
    lam = jnp.exp(jnp.sum(lf[0] * lf[1])) - jnp.exp(jnp.sum(lf[2] * lf[3])) + lam_init
    if k_past is None:
        o = prompt_diff_attention(q, k_sh, v_sh, lam)
    else:
        o = sample_diff_attention(q, k_past, v_past, k_sh, v_sh, lam)
    o = rms_norm(o, g_sub) * (1.0 - lam_init)
    return o.reshape(Bsz, T, V_DIM_B) @ w_o


def swiglu(x, w_gate, w_up, w_down):
    return (jax.nn.silu(x @ w_gate) * (x @ w_up)) @ w_down


def moe_swiglu(x, w_router, w_gate, w_up, w_down):
    Bsz, T, D = x.shape
    xt = x.reshape(-1, D)
    n_tok = xt.shape[0]
    logits = jnp.einsum('nd,de->ne', xt, w_router, preferred_element_type=jnp.float32)
    top_logit, top_e = lax.top_k(logits, TOP_K)
    gate = jax.nn.softmax(top_logit, axis=-1)
    n_rows = n_tok * TOP_K
    flat_e = top_e.reshape(-1)
    order = jnp.argsort(flat_e)
    e_sorted = flat_e[order]
    tok_sorted = (jnp.arange(n_rows) // TOP_K)[order]
    gate_sorted = gate.reshape(-1)[order]
    counts = jnp.bincount(flat_e, length=N_EXPERTS)
    padded = (counts + MOE_BLOCK - 1) // MOE_BLOCK * MOE_BLOCK
    start = jnp.cumsum(counts) - counts
    pad_end = jnp.cumsum(padded)
    pad_start = pad_end - padded
    dest = pad_start[e_sorted] + jnp.arange(n_rows) - start[e_sorted]
    n_blocks = -(-(n_rows + N_EXPERTS * (MOE_BLOCK - 1)) // MOE_BLOCK)
    buf_tok = jnp.full((n_blocks * MOE_BLOCK,), n_tok, jnp.int32).at[dest].set(tok_sorted)
    x_rows = jnp.concatenate([xt, jnp.zeros((1, D), xt.dtype)], 0)[buf_tok].reshape(n_blocks, MOE_BLOCK, D)
    block_e = jnp.minimum(jnp.sum((jnp.arange(n_blocks) * MOE_BLOCK)[:, None] >= pad_end[None, :], axis=1),
                          N_EXPERTS - 1)

    def expert_block(args):
        xb, e = args
        return swiglu(xb, w_gate[e], w_up[e], w_down[e])

    y_rows = lax.map(expert_block, (x_rows, block_e)).reshape(-1, D)[dest]
    y = jax.ops.segment_sum(y_rows * gate_sorted[:, None].astype(y_rows.dtype), tok_sorted, num_segments=n_tok)
    return y.reshape(Bsz, T, D)


def trunk(x, pos, conv_bufs, delta_states, k_past, v_past, p):
    h = x
    new_convs, new_states = [], []
    k_sh, v_sh = None, None
    for l in range(DEPTH):
        if l < N_A_LAYERS:
            mix, buf, st = gated_deltanet(h, conv_bufs[l], delta_states[l], p['w_in_a'][l], p['w_conv_a'][l],
                                          p['a_log_a'][l], p['dt_bias_a'][l], p['g_norm_a'][l], p['w_out_a'][l])
            new_convs.append(buf)
            new_states.append(st)
        else:
            j = l - N_A_LAYERS
            lam_init = 0.8 - 0.6 * math.exp(-0.3 * l)
            mix = diff_attention(h, pos, k_sh, v_sh, k_past, v_past, p['w_q_b'][j], p['lambda_b'][j],
                                 p['g_sub_b'][j], p['w_o_b'][j], lam_init)
        h = layer_norm(DEEPNORM_ALPHA * h + mix, p['ln_mix_g'][l], p['ln_mix_b'][l])
        if l % 2 == 0:
            f = swiglu(h, p['w_gate_d'][l // 2], p['w_up_d'][l // 2], p['w_down_d'][l // 2])
        else:
            f = moe_swiglu(h, p['w_router'][l // 2], p['w_gate_e'][l // 2], p['w_up_e'][l // 2], p['w_down_e'][l // 2])
        h = layer_norm(DEEPNORM_ALPHA * h + f, p['ln_ffn_g'][l], p['ln_ffn_b'][l])
        if l == N_A_LAYERS - 1:
            k_sh, v_sh = shared_kv(h, pos, p['w_kv'])
    return h, jnp.stack(new_convs), jnp.stack(new_states), k_sh, v_sh


def setup_inputs(seed: int = 0) -> dict:
    key = jax.random.key(seed)
    ks = jax.random.split(key, 32)

    def nrm(i, shape, scale):
        return jax.random.normal(ks[i], shape, jnp.float32) * scale

    n_pages = PAST_LEN // PAGE_SIZE
    n_pool = (5 * DEC_BATCH * n_pages + 3) // 4
    page_table = jax.random.permutation(ks[6], n_pool)[:DEC_BATCH * n_pages].reshape(DEC_BATCH, n_pages).astype(jnp.int32)
    dt = jnp.exp(jax.random.uniform(ks[10], (N_A_LAYERS, H_A), jnp.float32, math.log(1e-3), math.log(1e-1)))
    return {
        'x_prompt': nrm(0, (BATCH, SEQ, D_MODEL), 1.0),
        'x_sample': nrm(1, (DEC_BATCH, DEC_SEQ, D_MODEL), 1.0),
        'state_delta': nrm(2, (N_A_LAYERS, DEC_BATCH, H_A, DK_A, DV_A), DK_A ** -0.5),
        'state_conv': nrm(3, (N_A_LAYERS, DEC_BATCH, CONV_W - 1, QKV_DIM), 1.0),
        'cache_k': nrm(4, (n_pool, PAGE_SIZE, H_B, 2, D_HEAD_B), 1.0),
        'cache_v': nrm(5, (n_pool, PAGE_SIZE, H_B, 2 * D_HEAD_B), 1.0),
        'page_table': page_table,
        'w_in_a': nrm(7, (N_A_LAYERS, D_MODEL, IN_DIM_A), D_MODEL ** -0.5),
        'w_conv_a': nrm(8, (N_A_LAYERS, CONV_W, QKV_DIM), CONV_W ** -0.5),
        'a_log_a': jnp.log(jax.random.uniform(ks[9], (N_A_LAYERS, H_A), jnp.float32, 1.0, 16.0)),
        'dt_bias_a': dt + jnp.log(-jnp.expm1(-dt)),
        'g_norm_a': 1.0 + nrm(11, (N_A_LAYERS, DV_A), 0.02),
        'w_out_a': nrm(12, (N_A_LAYERS, V_DIM_A, D_MODEL), V_DIM_A ** -0.5 * DEEPNORM_BETA),
        'w_kv': nrm(13, (D_MODEL, K_DIM_B + V_DIM_B), D_MODEL ** -0.5),
        'w_q_b': nrm(14, (N_B_LAYERS, D_MODEL, K_DIM_B), D_MODEL ** -0.5),
        'lambda_b': nrm(15, (N_B_LAYERS, 4, D_HEAD_B), 0.1),
        'g_sub_b': 1.0 + nrm(16, (N_B_LAYERS, 2 * D_HEAD_B), 0.02),
        'w_o_b': nrm(17, (N_B_LAYERS, V_DIM_B, D_MODEL), V_DIM_B ** -0.5 * DEEPNORM_BETA),
        'ln_mix_g': 1.0 + nrm(18, (DEPTH, D_MODEL), 0.02),
        'ln_mix_b': nrm(19, (DEPTH, D_MODEL), 0.02),
        'ln_ffn_g': 1.0 + nrm(20, (DEPTH, D_MODEL), 0.02),
        'ln_ffn_b': nrm(21, (DEPTH, D_MODEL), 0.02),
        'w_gate_d': nrm(22, (N_DENSE, D_MODEL, F_DENSE), D_MODEL ** -0.5),
        'w_up_d': nrm(23, (N_DENSE, D_MODEL, F_DENSE), D_MODEL ** -0.5),
        'w_down_d': nrm(24, (N_DENSE, F_DENSE, D_MODEL), F_DENSE ** -0.5 * DEEPNORM_BETA),
        'w_router': nrm(25, (N_MOE, D_MODEL, N_EXPERTS), D_MODEL ** -0.5),
        'w_gate_e': nrm(26, (N_MOE, N_EXPERTS, D_MODEL, F_EXPERT), D_MODEL ** -0.5),
        'w_up_e': nrm(27, (N_MOE, N_EXPERTS, D_MODEL, F_EXPERT), D_MODEL ** -0.5),
        'w_down_e': nrm(28, (N_MOE, N_EXPERTS, F_EXPERT, D_MODEL), F_EXPERT ** -0.5 * DEEPNORM_BETA),
    }


def reference(x_prompt, x_sample, state_delta, state_conv, cache_k, cache_v, page_table,
              w_in_a, w_conv_a, a_log_a, dt_bias_a, g_norm_a, w_out_a, w_kv, w_q_b, lambda_b, g_sub_b, w_o_b,
              ln_mix_g, ln_mix_b, ln_ffn_g, ln_ffn_b, w_gate_d, w_up_d, w_down_d,
              w_router, w_gate_e, w_up_e, w_down_e):
    p = dict(w_in_a=w_in_a, w_conv_a=w_conv_a, a_log_a=a_log_a, dt_bias_a=dt_bias_a, g_norm_a=g_norm_a,
             w_out_a=w_out_a, w_kv=w_kv, w_q_b=w_q_b, lambda_b=lambda_b, g_sub_b=g_sub_b, w_o_b=w_o_b,
             ln_mix_g=ln_mix_g, ln_mix_b=ln_mix_b, ln_ffn_g=ln_ffn_g, ln_ffn_b=ln_ffn_b,
             w_gate_d=w_gate_d, w_up_d=w_up_d, w_down_d=w_down_d,
             w_router=w_router, w_gate_e=w_gate_e, w_up_e=w_up_e, w_down_e=w_down_e)
    Bp, Tp, _ = x_prompt.shape
    pos_p = jnp.arange(Tp)
    conv0 = jnp.zeros((N_A_LAYERS, Bp, CONV_W - 1, QKV_DIM), x_prompt.dtype)
    s0 = jnp.zeros((N_A_LAYERS, Bp, H_A, DK_A, DV_A), state_delta.dtype)
    y_prompt, conv_p, delta_p, k_p, v_p = trunk(x_prompt, pos_p, conv0, s0, None, None, p)
    Bs, Ts, _ = x_sample.shape
    n_pages = page_table.shape[1]
    past_len = n_pages * cache_k.shape[1]
    k_past = cache_k[page_table].reshape(Bs, past_len, H_B, 2, D_HEAD_B)
    v_past = cache_v[page_table].reshape(Bs, past_len, H_B, 2 * D_HEAD_B)
    pos_s = past_len + jnp.arange(Ts)
    y_sample, conv_s, delta_s, k_s, v_s = trunk(x_sample, pos_s, state_conv, state_delta, k_past, v_past, p)
    return (y_prompt, y_sample, delta_p, conv_p, k_p, v_p, delta_s, conv_s, k_s, v_s)
```

```python
import functools
import math

import jax
import jax.numpy as jnp
from jax import lax
from jax.experimental import pallas as pl
from jax.experimental.pallas import tpu as pltpu

F32 = jnp.float32
BF16 = jnp.bfloat16
HIGHEST = lax.Precision.HIGHEST

LANES = 128
SUBLANES = 8
BF16_ROWS = 16
VMEM_LIMIT = 56 * 1024 * 1024

D_MODEL = 1024
N_HEADS = 8
HEAD_DIM = 128
QKV_DIM = 3 * D_MODEL
CONV_W = 4
DELTA_CHUNK = 64
D_HEAD_B = 64
ROPE_THETA = 10000.0
N_EXPERTS = 8
DEPTH = 4
DEEPNORM_ALPHA = (2.0 * DEPTH) ** 0.25
LN_EPS = 1e-5
RMS_EPS = 1e-6
SAMPLE_PAD_T = 8


def _cparams(semantics):
    return pltpu.CompilerParams(dimension_semantics=semantics, vmem_limit_bytes=VMEM_LIMIT)


def _dot(a, b, precision=None):
    return jnp.dot(a, b, preferred_element_type=F32, precision=precision)


def _dot_nt(a, b, precision=None):
    return lax.dot_general(a, b, (((1,), (1,)), ((), ())), preferred_element_type=F32, precision=precision)


def _dot_tn(a, b, precision=None):
    return lax.dot_general(a, b, (((0,), (0,)), ((), ())), preferred_element_type=F32, precision=precision)


def _layer_norm(y, g, b):
    mu = jnp.mean(y, axis=-1, keepdims=True)
    d = y - mu
    var = jnp.mean(d * d, axis=-1, keepdims=True)
    return d * lax.rsqrt(var + LN_EPS) * g + b


def _silu(x):
    return x * jax.nn.sigmoid(x)


def _row_tile(m, pref):
    t = min(pref, m)
    while m % t:
        t //= 2
    return t


def _linear_kernel(x_ref, w_ref, *out_refs, widths, tn):
    x = x_ref[...]
    col = 0
    for o_ref, width in zip(out_refs, widths):
        for c in range(0, width, tn):
            cw = min(tn, width - c)
            o_ref[:, c:c + cw] = _dot(x, w_ref[:, col + c:col + c + cw]).astype(o_ref.dtype)
        col += width


def _linear(x, w, widths, dtypes, tm=256, tn=512):
    m, k = x.shape
    tm = _row_tile(m, tm)
    n = sum(widths)
    return pl.pallas_call(
        functools.partial(_linear_kernel, widths=tuple(widths), tn=tn),
        grid=(m // tm,),
        in_specs=[pl.BlockSpec((tm, k), lambda i: (i, 0)),
                  pl.BlockSpec((k, n), lambda i: (0, 0))],
        out_specs=[pl.BlockSpec((tm, wd), lambda i: (i, 0)) for wd in widths],
        out_shape=[jax.ShapeDtypeStruct((m, wd), dt) for wd, dt in zip(widths, dtypes)],
        compiler_params=_cparams(("parallel",)),
        name="linear",
    )(x, w)


def _rope(x, cos128, sin128):
    n = x.shape[1]
    reps = n // LANES
    cos_t = jnp.concatenate([cos128] * reps, axis=1)
    sin_t = jnp.concatenate([sin128] * reps, axis=1)
    lane = lax.broadcasted_iota(jnp.int32, x.shape, 1)
    first_half = (lane % D_HEAD_B) < (D_HEAD_B // 2)
    rot = jnp.where(first_half, pltpu.roll(x, n - D_HEAD_B // 2, 1), pltpu.roll(x, D_HEAD_B // 2, 1))
    return x * cos_t + rot * sin_t


def _q_proj_kernel(x_ref, w_ref, cos_ref, sin_ref, q_ref, *, scale):
    q = _dot(x_ref[...], w_ref[...])
    q_ref[...] = (_rope(q, cos_ref[...], sin_ref[...]) * scale).astype(q_ref.dtype)


def _q_proj(x, w, cos, sin, out_dtype, tm=256):
    m, k = x.shape
    tm = _row_tile(m, tm)
    return pl.pallas_call(
        functools.partial(_q_proj_kernel, scale=D_HEAD_B ** -0.5),
        grid=(m // tm,),
        in_specs=[pl.BlockSpec((tm, k), lambda i: (i, 0)),
                  pl.BlockSpec((k, D_MODEL), lambda i: (0, 0)),
                  pl.BlockSpec((tm, LANES), lambda i: (i, 0)),
                  pl.BlockSpec((tm, LANES), lambda i: (i, 0))],
        out_specs=pl.BlockSpec((tm, D_MODEL), lambda i: (i, 0)),
        out_shape=jax.ShapeDtypeStruct((m, D_MODEL), out_dtype),
        compiler_params=_cparams(("parallel",)),
        name="q_proj",
    )(x, w, cos, sin)


def _kv_proj_kernel(x_ref, w_ref, cos_ref, sin_ref, k_ref, v_ref, kb_ref, vb_ref):
    x = x_ref[...]
    k = _rope(_dot(x, w_ref[:, :D_MODEL]), cos_ref[...], sin_ref[...])
    v = _dot(x, w_ref[:, D_MODEL:])
    k_ref[...] = k
    v_ref[...] = v
    kb_ref[...] = k.astype(BF16)
    vb_ref[...] = v.astype(BF16)


def _kv_proj(x, w, cos, sin, tm=256):
    m, k = x.shape
    tm = _row_tile(m, tm)
    row = lambda i: (i, 0)
    return pl.pallas_call(
        _kv_proj_kernel,
        grid=(m // tm,),
        in_specs=[pl.BlockSpec((tm, k), row),
                  pl.BlockSpec((k, 2 * D_MODEL), lambda i: (0, 0)),
                  pl.BlockSpec((tm, LANES), row),
                  pl.BlockSpec((tm, LANES), row)],
        out_specs=[pl.BlockSpec((tm, D_MODEL), row)] * 4,
        out_shape=[jax.ShapeDtypeStruct((m, D_MODEL), F32)] * 2 + [jax.ShapeDtypeStruct((m, D_MODEL), BF16)] * 2,
        compiler_params=_cparams(("parallel",)),
        name="kv_proj",
    )(x, w, cos, sin)


def _conv_gate_kernel(x_ref, p_ref, ab_ref, cw_ref, alog_ref, dtb_ref, q_ref, k_ref, v_ref, gb_ref,
                      *, seg, t_real, chunk, blocks_per_seq):
    tm = x_ref.shape[0]
    row = lax.broadcasted_iota(jnp.int32, (tm, 1), 0)
    tloc = row % seg if seg < tm else row
    valid = tloc < t_real
    if seg == tm:
        not_first = (pl.program_id(0) % blocks_per_seq) != 0
        row8 = lax.broadcasted_iota(jnp.int32, (SUBLANES, 1), 0)

    for c in range(QKV_DIM // LANES):
        cs = slice(c * LANES, (c + 1) * LANES)
        xc = x_ref[:, cs]
        acc = cw_ref[CONV_W - 1:CONV_W, cs] * xc
        for j in range(1, CONV_W):
            sh = pltpu.roll(xc, j, 0)
            if seg == tm:
                halo = jnp.where(not_first, p_ref[:, cs], 0.0)
                head = jnp.where(row8 < j, pltpu.roll(halo, j, 0), sh[:SUBLANES])
                sh = jnp.concatenate([head, sh[SUBLANES:]], axis=0)
            else:
                prev = pltpu.roll(p_ref[:, cs], tm + j - SUBLANES, 0)
                sh = jnp.where(tloc >= j, sh, prev)
            acc = acc + cw_ref[CONV_W - 1 - j:CONV_W - j, cs] * sh
        y = _silu(acc)
        if c < 2 * N_HEADS:
            y = y * lax.rsqrt(jnp.sum(y * y, axis=-1, keepdims=True) + 1e-6)
        if c < N_HEADS:
            y = y * (HEAD_DIM ** -0.5)
        y = jnp.where(valid, y, 0.0)
        if c < N_HEADS:
            q_ref[:, cs] = y
        elif c < 2 * N_HEADS:
            k_ref[:, (c - N_HEADS) * LANES:(c - N_HEADS + 1) * LANES] = y
        else:
            v_ref[:, (c - 2 * N_HEADS) * LANES:(c - 2 * N_HEADS + 1) * LANES] = y

    ab = ab_ref[...]
    z = ab + dtb_ref[...]
    softplus = jnp.maximum(z, 0.0) + jnp.log1p(jnp.exp(-jnp.abs(z)))
    g = jnp.where(valid, -jnp.exp(alog_ref[...]) * softplus, 0.0)
    beta = jnp.where(valid, jax.nn.sigmoid(ab), 0.0)
    ri = lax.broadcasted_iota(jnp.int32, (tm, tm), 0)
    ci = lax.broadcasted_iota(jnp.int32, (tm, tm), 1)
    tril = jnp.where((ri // chunk == ci // chunk) & (ci <= ri), 1.0, 0.0)
    g_cum = _dot(tril, g, precision=HIGHEST)
    lane = lax.broadcasted_iota(jnp.int32, (tm, LANES), 1)
    gb_ref[...] = jnp.where(lane < N_HEADS, g_cum, beta)


def _conv_gate(qkv_pre, prev, ab, conv_w, a_log, dt_bias, *, seq_len, t_real, chunk, tm=256):
    m = qkv_pre.shape[0]
    if prev is None:
        tm = _row_tile(seq_len, tm)
        seg = tm
        blocks_per_seq = seq_len // tm
        tiles_per_block = tm // SUBLANES
        p_arr = qkv_pre
        p_spec = pl.BlockSpec((SUBLANES, QKV_DIM), lambda i: (jnp.maximum(i * tiles_per_block - 1, 0), 0))
    else:
        tm = _row_tile(m, tm)
        seg = seq_len
        blocks_per_seq = 1
        p_arr = prev
        p_spec = pl.BlockSpec((tm, QKV_DIM), lambda i: (i, 0))
    assert seg % chunk == 0 or chunk % seg == 0
    row = lambda i: (i, 0)
    const = lambda i: (0, 0)
    cw = jnp.zeros((SUBLANES, QKV_DIM), F32).at[:CONV_W].set(conv_w)
    alog = jnp.zeros((1, LANES), F32).at[0, :N_HEADS].set(a_log)
    dtb = jnp.zeros((1, LANES), F32).at[0, :N_HEADS].set(dt_bias)
    return pl.pallas_call(
        functools.partial(_conv_gate_kernel, seg=seg, t_real=t_real, chunk=min(chunk, seg),
                          blocks_per_seq=blocks_per_seq),
        grid=(m // tm,),
        in_specs=[pl.BlockSpec((tm, QKV_DIM), row), p_spec, pl.BlockSpec((tm, LANES), row),
                  pl.BlockSpec((SUBLANES, QKV_DIM), const), pl.BlockSpec((1, LANES), const),
                  pl.BlockSpec((1, LANES), const)],
        out_specs=[pl.BlockSpec((tm, D_MODEL), row)] * 3 + [pl.BlockSpec((tm, LANES), row)],
        out_shape=[jax.ShapeDtypeStruct((m, D_MODEL), F32)] * 3 + [jax.ShapeDtypeStruct((m, LANES), F32)],
        compiler_params=_cparams(("parallel",)),
        name="conv_gate",
    )(qkv_pre, p_arr, ab, cw, alog, dtb)


def _delta_prepare(qs, ks, vs, gs, bs):
    n = len(qs)
    c = qs[0].shape[0]
    ri = lax.broadcasted_iota(jnp.int32, (c, c), 0)
    ci = lax.broadcasted_iota(jnp.int32, (c, c), 1)
    causal = ci <= ri
    strict = ci < ri
    eye = ci == ri
    decay, kb, k_b, pw = [], [], [], []
    for i in range(n):
        g_row = jnp.sum(jnp.where(eye, gs[i], 0.0), axis=0, keepdims=True)
        decay.append(jnp.where(causal, jnp.exp(jnp.where(causal, gs[i] - g_row, 0.0)), 0.0))
        kb.append(ks[i] * bs[i])
        k_b.append(ks[i].astype(BF16))
    for i in range(n):
        pw.append(-jnp.where(strict, _dot_nt(kb[i].astype(BF16), k_b[i]) * decay[i], 0.0))
    inv_m1 = list(pw)
    for _ in range(int(math.log2(c)) - 1):
        for i in range(n):
            pw_b = pw[i].astype(BF16)
            pw[i] = _dot(pw_b, pw_b)
        for i in range(n):
            inv_m1[i] = inv_m1[i] + pw[i] + _dot(inv_m1[i].astype(BF16), pw[i].astype(BF16))
    out = []
    for i in range(n):
        e_g = jnp.exp(gs[i])
        rhs = jnp.concatenate([vs[i] * bs[i], kb[i] * e_g], axis=1)
        sol = rhs + _dot(inv_m1[i].astype(BF16), rhs.astype(BF16))
        attn = _dot_nt(qs[i].astype(BF16), k_b[i]) * decay[i]
        g_last = gs[i][c - 1:c, :]
        out.append(dict(u=sol[:, :HEAD_DIM], w=sol[:, HEAD_DIM:].astype(BF16), attn=attn.astype(BF16),
                        q_dec=(qs[i] * e_g).astype(BF16),
                        k_dec=(ks[i] * jnp.exp(g_last - gs[i])).astype(BF16), decay_last=jnp.exp(g_last)))
    return out


def _delta_apply(prep, states):
    n = len(prep)
    sb = [s.astype(BF16) for s in states]
    v_new = [(prep[i]["u"] - _dot(prep[i]["w"], sb[i])).astype(BF16) for i in range(n)]
    outs = [_dot(prep[i]["q_dec"], sb[i]) + _dot(prep[i]["attn"], v_new[i]) for i in range(n)]
    new_states = [states[i] * prep[i]["decay_last"] + _dot_tn(prep[i]["k_dec"], v_new[i]) for i in range(n)]
    return outs, new_states


def _delta_kernel(*refs, chunk, has_s0):
    if has_s0:
        q_ref, k_ref, v_ref, gb_ref, s0_ref, o_ref, s_ref = refs
    else:
        q_ref, k_ref, v_ref, gb_ref, o_ref, s_ref = refs
        s0_ref = None
    rows = q_ref.shape[0]

    @pl.when(pl.program_id(1) == 0)
    def _init():
        if has_s0:
            s_ref[...] = s0_ref[...]
        else:
            s_ref[...] = jnp.zeros_like(s_ref)

    n_chunks = rows // chunk
    qs, ks, vs, gs, bs = [], [], [], [], []
    for ci in range(n_chunks):
        rs = slice(ci * chunk, (ci + 1) * chunk)
        gb = gb_ref[rs, :]
        for h in range(N_HEADS):
            hs = slice(h * HEAD_DIM, (h + 1) * HEAD_DIM)
            qs.append(q_ref[rs, hs])
            ks.append(k_ref[rs, hs])
            vs.append(v_ref[rs, hs])
            gs.append(gb[:, h:h + 1])
            bs.append(gb[:, N_HEADS + h:N_HEADS + h + 1])
    prep = _delta_prepare(qs, ks, vs, gs, bs)
    states = [s_ref[0, h] for h in range(N_HEADS)]
    for ci in range(n_chunks):
        outs, states = _delta_apply(prep[ci * N_HEADS:(ci + 1) * N_HEADS], states)
        for h in range(N_HEADS):
            o_ref[ci * chunk:(ci + 1) * chunk, h * HEAD_DIM:(h + 1) * HEAD_DIM] = outs[h]
    for h in range(N_HEADS):
        s_ref[0, h] = states[h]


def _delta_rule(q, k, v, gb, s0, *, n_seq, seq_len, chunk, rows_per_step=128):
    m = q.shape[0]
    chunk = min(chunk, seq_len)
    rows = _row_tile(seq_len, max(rows_per_step, chunk))
    steps = seq_len // rows
    row = lambda s, i: (s * steps + i, 0)
    st = lambda s, i: (s, 0, 0, 0)
    in_specs = [pl.BlockSpec((rows, D_MODEL), row)] * 3 + [pl.BlockSpec((rows, LANES), row)]
    args = [q, k, v, gb]
    if s0 is not None:
        in_specs.append(pl.BlockSpec((1, N_HEADS, HEAD_DIM, HEAD_DIM), st))
        args.append(s0)
    return pl.pallas_call(
        functools.partial(_delta_kernel, chunk=chunk, has_s0=s0 is not None),
        grid=(n_seq, steps),
        in_specs=in_specs,
        out_specs=[pl.BlockSpec((rows, D_MODEL), row), pl.BlockSpec((1, N_HEADS, HEAD_DIM, HEAD_DIM), st)],
        out_shape=[jax.ShapeDtypeStruct((m, D_MODEL), F32),
                   jax.ShapeDtypeStruct((n_seq, N_HEADS, HEAD_DIM, HEAD_DIM), F32)],
        compiler_params=_cparams(("parallel", "arbitrary")),
        name="delta_rule",
    )(*args)


def _mix_out_kernel(*refs, gated, post_scale):
    if gated:
        o_ref, z_ref, h_ref, gn_ref, w_ref, lg_ref, lb_ref, hf_ref, hb_ref = refs
    else:
        o_ref, h_ref, gn_ref, w_ref, lg_ref, lb_ref, hf_ref, hb_ref = refs
    parts = []
    for hd in range(N_HEADS):
        hs = slice(hd * HEAD_DIM, (hd + 1) * HEAD_DIM)
        o = o_ref[:, hs]
        y = o * lax.rsqrt(jnp.mean(o * o, axis=-1, keepdims=True) + RMS_EPS) * gn_ref[:, hs]
        if gated:
            y = y * _silu(z_ref[:, hs].astype(F32))
        else:
            y = y * post_scale
        parts.append(y.astype(BF16))
    mix = _dot(jnp.concatenate(parts, axis=1), w_ref[...])
    hn = _layer_norm(DEEPNORM_ALPHA * h_ref[...] + mix, lg_ref[...], lb_ref[...])
    hf_ref[...] = hn
    hb_ref[...] = hn.astype(BF16)


def _mix_out(o, z, h, g_norm, w, ln_g, ln_b, *, post_scale=1.0, tm=256):
    m = o.shape[0]
    tm = _row_tile(m, tm)
    row = lambda i: (i, 0)
    const = lambda i: (0, 0)
    gated = z is not None
    gn = jnp.tile(g_norm.reshape(1, HEAD_DIM), (1, N_HEADS))
    args = [o] + ([z] if gated else []) + [h, gn, w, ln_g.reshape(1, -1), ln_b.reshape(1, -1)]
    in_specs = ([pl.BlockSpec((tm, D_MODEL), row)] * (3 if gated else 2)
                + [pl.BlockSpec((1, D_MODEL), const), pl.BlockSpec((D_MODEL, D_MODEL), const),
                   pl.BlockSpec((1, D_MODEL), const), pl.BlockSpec((1, D_MODEL), const)])
    return pl.pallas_call(
        functools.partial(_mix_out_kernel, gated=gated, post_scale=post_scale),
        grid=(m // tm,),
        in_specs=in_specs,
        out_specs=[pl.BlockSpec((tm, D_MODEL), row)] * 2,
        out_shape=[jax.ShapeDtypeStruct((m, D_MODEL), F32), jax.ShapeDtypeStruct((m, D_MODEL), BF16)],
        compiler_params=_cparams(("parallel",)),
        name="mix_out",
    )(*args)


def _dense_ffn_kernel(xb_ref, h_ref, wg_ref, wu_ref, wd_ref, lg_ref, lb_ref, hf_ref, hb_ref, *, tf):
    x = xb_ref[...]
    f_dim = wg_ref.shape[1]
    acc = DEEPNORM_ALPHA * h_ref[...]
    for c in range(0, f_dim, tf):
        cw = min(tf, f_dim - c)
        g = _dot(x, wg_ref[:, c:c + cw])
        u = _dot(x, wu_ref[:, c:c + cw])
        acc = acc + _dot((_silu(g) * u).astype(BF16), wd_ref[c:c + cw, :])
    hn = _layer_norm(acc, lg_ref[...], lb_ref[...])
    hf_ref[...] = hn
    hb_ref[...] = hn.astype(BF16)


def _dense_ffn(xb, h, wg, wu, wd, ln_g, ln_b, *, tm=256, tf=256):
    m = xb.shape[0]
    f_dim = wg.shape[1]
    tm = _row_tile(m, tm)
    row = lambda i: (i, 0)
    const = lambda i: (0, 0)
    return pl.pallas_call(
        functools.partial(_dense_ffn_kernel, tf=tf),
        grid=(m // tm,),
        in_specs=[pl.BlockSpec((tm, D_MODEL), row), pl.BlockSpec((tm, D_MODEL), row),
                  pl.BlockSpec((D_MODEL, f_dim), const), pl.BlockSpec((D_MODEL, f_dim), const),
                  pl.BlockSpec((f_dim, D_MODEL), const),
                  pl.BlockSpec((1, D_MODEL), const), pl.BlockSpec((1, D_MODEL), const)],
        out_specs=[pl.BlockSpec((tm, D_MODEL), row)] * 2,
        out_shape=[jax.ShapeDtypeStruct((m, D_MODEL), F32), jax.ShapeDtypeStruct((m, D_MODEL), BF16)],
        compiler_params=_cparams(("parallel",)),
        name="dense_ffn",
    )(xb, h, wg, wu, wd, ln_g.reshape(1, -1), ln_b.reshape(1, -1))


MOE_TM = 256
MOE_BM = 512
MOE_ALIGN = BF16_ROWS


def _router_kernel(h_ref, wr_ref, a_ref, r_ref, g_ref, cnt_ref):
    tm = h_ref.shape[0]
    logits = _dot_nt(wr_ref[...], h_ref[...], precision=HIGHEST)
    e_idx = lax.broadcasted_iota(jnp.int32, logits.shape, 0).astype(F32)
    m1 = jnp.max(logits, axis=0, keepdims=True)
    i1 = jnp.min(jnp.where(logits == m1, e_idx, float(N_EXPERTS)), axis=0, keepdims=True)
    first = e_idx == i1
    rest = jnp.where(first, -jnp.inf, logits)
    m2 = jnp.max(rest, axis=0, keepdims=True)
    i2 = jnp.min(jnp.where(rest == m2, e_idx, float(N_EXPERTS)), axis=0, keepdims=True)
    second = e_idx == i2
    ex = jnp.exp(m2 - m1)
    g1 = 1.0 / (1.0 + ex)
    g2 = ex / (1.0 + ex)
    routed = jnp.where(first | second, 1.0, 0.0)
    ji = lax.broadcasted_iota(jnp.int32, (tm, tm), 0)
    ii = lax.broadcasted_iota(jnp.int32, (tm, tm), 1)
    before = jnp.where(ji < ii, 1.0, 0.0).astype(BF16)
    a_ref[...] = routed
    r_ref[...] = _dot(routed.astype(BF16), before)
    g_ref[...] = jnp.where(first, g1, 0.0) + jnp.where(second, g2, 0.0)
    cnt_ref[0] = jnp.broadcast_to(jnp.sum(routed, axis=1, keepdims=True), (N_EXPERTS, LANES))


def _router(h, w_router_t, tm):
    m = h.shape[0]
    nb = m // tm
    col = lambda i: (0, i)
    return pl.pallas_call(
        _router_kernel,
        grid=(nb,),
        in_specs=[pl.BlockSpec((tm, D_MODEL), lambda i: (i, 0)),
                  pl.BlockSpec((N_EXPERTS, D_MODEL), lambda i: (0, 0))],
        out_specs=[pl.BlockSpec((N_EXPERTS, tm), col)] * 3
                  + [pl.BlockSpec((1, N_EXPERTS, LANES), lambda i: (i, 0, 0))],
        out_shape=[jax.ShapeDtypeStruct((N_EXPERTS, m), F32)] * 3
                  + [jax.ShapeDtypeStruct((nb, N_EXPERTS, LANES), F32)],
        compiler_params=_cparams(("parallel",)),
        name="moe_router",
    )(h, w_router_t)


def _dispatch_kernel(off_ref, xb_ref, a_ref, r_ref, rows_in_ref, rows_ref, slab_ref, sem_ref):
    del rows_in_ref
    b = pl.program_id(0)
    e = pl.program_id(1)
    step = b * N_EXPERTS + e
    n_steps = pl.num_programs(0) * N_EXPERTS
    slot = step % 2
    tm = xb_ref.shape[0]

    def slab_copy(which_slot, which_step):
        off = pl.multiple_of(off_ref[which_step], MOE_ALIGN)
        return pltpu.make_async_copy(slab_ref.at[which_slot], rows_ref.at[pl.ds(off, tm)], sem_ref.at[which_slot])

    @pl.when(step >= 2)
    def _wait_slot_free():
        slab_copy(slot, step - 2).wait()

    a_row = a_ref[pl.ds(e, 1), :]
    r_row = r_ref[pl.ds(e, 1), :]
    r_idx = lax.broadcasted_iota(jnp.int32, (tm, tm), 0).astype(F32)
    pick = jnp.where((r_idx == r_row) & (a_row > 0.0), 1.0, 0.0).astype(BF16)
    slab_ref[slot] = _dot(pick, xb_ref[...]).astype(BF16)
    slab_copy(slot, step).start()

    @pl.when(step == n_steps - 1)
    def _drain():
        slab_copy(slot, step).wait()

        @pl.when(n_steps >= 2)
        def _():
            slab_copy(1 - slot, step - 1).wait()


def _dispatch(xb, a_t, r_t, off, n_rows, tm):
    m = xb.shape[0]
    nb = m // tm
    rows_init = jnp.zeros((n_rows, D_MODEL), BF16)
    grid_spec = pltpu.PrefetchScalarGridSpec(
        num_scalar_prefetch=1,
        grid=(nb, N_EXPERTS),
        in_specs=[pl.BlockSpec((tm, D_MODEL), lambda b, e, off: (b, 0)),
                  pl.BlockSpec((N_EXPERTS, tm), lambda b, e, off: (0, b)),
                  pl.BlockSpec((N_EXPERTS, tm), lambda b, e, off: (0, b)),
                  pl.BlockSpec(memory_space=pl.ANY)],
        out_specs=pl.BlockSpec(memory_space=pl.ANY),
        scratch_shapes=[pltpu.VMEM((2, tm, D_MODEL), BF16), pltpu.SemaphoreType.DMA((2,))],
    )
    return pl.pallas_call(
        _dispatch_kernel,
        grid_spec=grid_spec,
        out_shape=jax.ShapeDtypeStruct((n_rows, D_MODEL), BF16),
        input_output_aliases={4: 0},
        compiler_params=_cparams(("arbitrary", "arbitrary")),
        name="moe_dispatch",
    )(off, xb, a_t, r_t, rows_init)


def _expert_ffn_kernel(be_ref, valid_ref, x_ref, wg_ref, wu_ref, wd_ref, y_ref, acc_ref):
    del be_ref
    g_idx = pl.program_id(0)
    f = pl.program_id(1)
    nf = pl.num_programs(1)
    is_valid = valid_ref[g_idx] > 0

    @pl.when(is_valid)
    def _compute():
        x = x_ref[...]
        g = _dot(x, wg_ref[0])
        u = _dot(x, wu_ref[0])
        part = _dot((_silu(g) * u).astype(BF16), wd_ref[0])

        @pl.when(f == 0)
        def _():
            acc_ref[...] = part

        @pl.when(f > 0)
        def _():
            acc_ref[...] += part

        @pl.when(f == nf - 1)
        def _():
            y_ref[...] = acc_ref[...].astype(y_ref.dtype)

    @pl.when(jnp.logical_not(is_valid) & (f == nf - 1))
    def _skip():
        y_ref[...] = jnp.zeros_like(y_ref)


def _expert_ffn(x_rows, blk_e, valid, wg, wu, wd, bm, tf):
    n_rows = x_rows.shape[0]
    f_dim = wg.shape[2]
    nf = f_dim // tf
    grid_spec = pltpu.PrefetchScalarGridSpec(
        num_scalar_prefetch=2,
        grid=(n_rows // bm, nf),
        in_specs=[pl.BlockSpec((bm, D_MODEL), lambda g, f, be, va: (g, 0)),
                  pl.BlockSpec((1, D_MODEL, tf), lambda g, f, be, va: (be[g], 0, f)),
                  pl.BlockSpec((1, D_MODEL, tf), lambda g, f, be, va: (be[g], 0, f)),
                  pl.BlockSpec((1, tf, D_MODEL), lambda g, f, be, va: (be[g], f, 0))],
        out_specs=pl.BlockSpec((bm, D_MODEL), lambda g, f, be, va: (g, 0)),
        scratch_shapes=[pltpu.VMEM((bm, D_MODEL), F32)],
    )
    return pl.pallas_call(
        _expert_ffn_kernel,
        grid_spec=grid_spec,
        out_shape=jax.ShapeDtypeStruct((n_rows, D_MODEL), BF16),
        compiler_params=_cparams(("parallel", "arbitrary")),
        name="moe_expert_ffn",
    )(blk_e, valid, x_rows, wg, wu, wd)


def _combine_kernel(off_ref, h_ref, a_ref, r_ref, g_ref, lg_ref, lb_ref, rows_ref, hf_ref, hb_ref,
                    slab_ref, sem_ref, acc_ref):
    b = pl.program_id(0)
    e = pl.program_id(1)
    step = b * N_EXPERTS + e
    n_steps = pl.num_programs(0) * N_EXPERTS
    slot = step % 2
    tm = h_ref.shape[0]

    def slab_copy(which_slot, which_step):
        off = pl.multiple_of(off_ref[which_step], MOE_ALIGN)
        return pltpu.make_async_copy(rows_ref.at[pl.ds(off, tm)], slab_ref.at[which_slot], sem_ref.at[which_slot])

    @pl.when(step == 0)
    def _prime():
        slab_copy(0, 0).start()

    @pl.when(step + 1 < n_steps)
    def _prefetch():
        slab_copy(1 - slot, step + 1).start()

    @pl.when(e == 0)
    def _init():
        acc_ref[...] = DEEPNORM_ALPHA * h_ref[...]

    lane = lax.broadcasted_iota(jnp.int32, (tm, N_EXPERTS), 1)
    sel = lane == e
    a_col = jnp.sum(jnp.where(sel, a_ref[...], 0.0), axis=1, keepdims=True)
    r_col = jnp.sum(jnp.where(sel, r_ref[...], 0.0), axis=1, keepdims=True)
    g_col = jnp.sum(jnp.where(sel, g_ref[...], 0.0), axis=1, keepdims=True)
    r_idx = lax.broadcasted_iota(jnp.int32, (tm, tm), 1).astype(F32)
    pick = jnp.where((r_idx == r_col) & (a_col > 0.0), 1.0, 0.0).astype(BF16)
    slab_copy(slot, step).wait()
    acc_ref[...] += g_col * _dot(pick, slab_ref[slot])

    @pl.when(e == N_EXPERTS - 1)
    def _finish():
        hn = _layer_norm(acc_ref[...], lg_ref[...], lb_ref[...])
        hf_ref[...] = hn
        hb_ref[...] = hn.astype(BF16)


def _combine(h, a_c, r_c, g_c, ln_g, ln_b, y_rows, off, tm):
    m = h.shape[0]
    nb = m // tm
    blk = lambda b, e, off: (b, 0)
    const = lambda b, e, off: (0, 0)
    grid_spec = pltpu.PrefetchScalarGridSpec(
        num_scalar_prefetch=1,
        grid=(nb, N_EXPERTS),
        in_specs=[pl.BlockSpec((tm, D_MODEL), blk),
                  pl.BlockSpec((tm, N_EXPERTS), blk), pl.BlockSpec((tm, N_EXPERTS), blk),
                  pl.BlockSpec((tm, N_EXPERTS), blk),
                  pl.BlockSpec((1, D_MODEL), const), pl.BlockSpec((1, D_MODEL), const),
                  pl.BlockSpec(memory_space=pl.ANY)],
        out_specs=[pl.BlockSpec((tm, D_MODEL), blk)] * 2,
        scratch_shapes=[pltpu.VMEM((2, tm, D_MODEL), BF16), pltpu.SemaphoreType.DMA((2,)),
                        pltpu.VMEM((tm, D_MODEL), F32)],
    )
    return pl.pallas_call(
        _combine_kernel,
        grid_spec=grid_spec,
        out_shape=[jax.ShapeDtypeStruct((m, D_MODEL), F32), jax.ShapeDtypeStruct((m, D_MODEL), BF16)],
        compiler_params=_cparams(("arbitrary", "arbitrary")),
        name="moe_combine",
    )(off, h, a_c, r_c, g_c, ln_g.reshape(1, -1), ln_b.reshape(1, -1), y_rows)


def _moe_ffn(h, hb, w_router, wg, wu, wd, ln_g, ln_b):
    m = h.shape[0]
    tm = _row_tile(m, MOE_TM)
    nb = m // tm
    bm = MOE_BM
    f_dim = wg.shape[2]
    tf = f_dim // 2 if (f_dim // 2) % 256 == 0 else f_dim
    a_t, r_t, g_t, cnts = _router(h, w_router.T, tm)

    cnt = cnts[:, :, 0].astype(jnp.int32)
    padded = (cnt + MOE_ALIGN - 1) // MOE_ALIGN * MOE_ALIGN
    base = jnp.cumsum(padded, axis=0) - padded
    used = jnp.sum(padded, axis=0)
    cap = (used + tm + bm - 1) // bm * bm
    region = jnp.cumsum(cap) - cap
    off = (region[None, :] + base).reshape(-1).astype(jnp.int32)
    max_rows = 2 * m + nb * N_EXPERTS * (MOE_ALIGN - 1) + N_EXPERTS * (tm + bm)
    n_rows = (max_rows + bm - 1) // bm * bm
    g_idx = jnp.arange(n_rows // bm, dtype=jnp.int32)
    region_blk = region // bm
    blk_e = (jnp.sum(g_idx[:, None] >= region_blk[None, :], axis=1) - 1).astype(jnp.int32)
    n_blk = (used + bm - 1) // bm
    valid = ((g_idx - region_blk[blk_e]) < n_blk[blk_e]).astype(jnp.int32)

    x_rows = _dispatch(hb, a_t, r_t, off, n_rows, tm)
    y_rows = _expert_ffn(x_rows, blk_e, valid, wg, wu, wd, bm, tf)
    return _combine(h, a_t.T, r_t.T, g_t.T, ln_g, ln_b, y_rows, off, tm)


def _lambda_value(lam_ref, lam_init):
    s1 = jnp.sum(lam_ref[0:1, :] * lam_ref[1:2, :], axis=1, keepdims=True)
    s2 = jnp.sum(lam_ref[2:3, :] * lam_ref[3:4, :], axis=1, keepdims=True)
    return jnp.exp(s1) - jnp.exp(s2) + lam_init


def _attn_prompt_kernel(qi_ref, kj_ref, lam_ref, q_ref, k_ref, v_ref, o_ref, qq_ref, m_ref, l_ref, acc_ref,
                        *, lam_init):
    p = pl.program_id(2)
    i = qi_ref[p]
    j = kj_ref[p]
    tq = q_ref.shape[0]
    tk = k_ref.shape[0]

    @pl.when(j == 0)
    def _init():
        q = q_ref[...]
        lane = lax.broadcasted_iota(jnp.int32, q.shape, 1)
        zero = jnp.zeros_like(q)
        qq_ref[0:tq, :] = jnp.where(lane < D_HEAD_B, q, zero)
        qq_ref[tq:2 * tq, :] = jnp.where(lane >= D_HEAD_B, q, zero)
        m_ref[...] = jnp.full_like(m_ref, -jnp.inf)
        l_ref[...] = jnp.zeros_like(l_ref)
        acc_ref[...] = jnp.zeros_like(acc_ref)

    def step(masked):
        s = _dot_nt(qq_ref[...], k_ref[...])
        if masked:
            qpos = lax.broadcasted_iota(jnp.int32, s.shape, 0) % tq
            kpos = lax.broadcasted_iota(jnp.int32, s.shape, 1)
            s = jnp.where(kpos <= qpos, s, -jnp.inf)
        m_prev = m_ref[...]
        m_new = jnp.maximum(m_prev, jnp.max(s, axis=1, keepdims=True))
        alpha = jnp.exp(m_prev - m_new)
        pr = jnp.exp(s - jnp.concatenate([m_new] * (tk // LANES), axis=1))
        l_ref[...] = alpha * l_ref[...] + jnp.sum(pr, axis=1, keepdims=True)
        acc_ref[...] = alpha * acc_ref[...] + _dot(pr.astype(BF16), v_ref[...])
        m_ref[...] = m_new

    @pl.when(j < i)
    def _full():
        step(False)

    @pl.when(j == i)
    def _diag():
        step(True)
        lam = _lambda_value(lam_ref, lam_init)
        o = acc_ref[...] / l_ref[...]
        o_ref[...] = o[:tq] - lam * o[tq:]


def _attn_prompt(q, k, v, lam_p, lam_init, *, n_seq, seq_len, tq=512):
    m = q.shape[0]
    tq = _row_tile(seq_len, tq)
    nq = seq_len // tq
    pairs = [(i, j) for i in range(nq) for j in range(i + 1)]
    qi = jnp.asarray([p[0] for p in pairs], jnp.int32)
    kj = jnp.asarray([p[1] for p in pairs], jnp.int32)
    lam_pad = jnp.zeros((SUBLANES, LANES), F32).at[:4, :D_HEAD_B].set(lam_p)
    grid_spec = pltpu.PrefetchScalarGridSpec(
        num_scalar_prefetch=2,
        grid=(n_seq, N_HEADS, len(pairs)),
        in_specs=[pl.BlockSpec((SUBLANES, LANES), lambda b, h, p, qi, kj: (0, 0)),
                  pl.BlockSpec((tq, HEAD_DIM), lambda b, h, p, qi, kj: (b * nq + qi[p], h)),
                  pl.BlockSpec((tq, HEAD_DIM), lambda b, h, p, qi, kj: (b * nq + kj[p], h)),
                  pl.BlockSpec((tq, HEAD_DIM), lambda b, h, p, qi, kj: (b * nq + kj[p], h))],
        out_specs=pl.BlockSpec((tq, HEAD_DIM), lambda b, h, p, qi, kj: (b * nq + qi[p], h)),
        scratch_shapes=[pltpu.VMEM((2 * tq, HEAD_DIM), BF16), pltpu.VMEM((2 * tq, LANES), F32),
                        pltpu.VMEM((2 * tq, LANES), F32), pltpu.VMEM((2 * tq, HEAD_DIM), F32)],
    )
    assert tq % LANES == 0
    return pl.pallas_call(
        functools.partial(_attn_prompt_kernel, lam_init=lam_init),
        grid_spec=grid_spec,
        out_shape=jax.ShapeDtypeStruct((m, D_MODEL), F32),
        compiler_params=_cparams(("parallel", "parallel", "arbitrary")),
        name="attn_prompt",
    )(qi, kj, lam_pad, q, k, v)


def _attn_sample_kernel(*refs, pages_per_step, t_real, lam_init):
    pp = pages_per_step
    pt_ref, lam_ref, q_ref, kn_ref, vn_ref = refs[:5]
    k_refs = refs[5:5 + pp]
    v_refs = refs[5 + pp:5 + 2 * pp]
    o_ref, qbd_ref, kpad_ref, vpad_ref, m_ref, l_ref, acc_ref = refs[5 + 2 * pp:]
    del pt_ref
    j = pl.program_id(1)
    n_rows = qbd_ref.shape[0]
    page = kpad_ref.shape[0]

    def update(s_list, v_list):
        m_prev = m_ref[...]
        m_cur = jnp.max(s_list[0], axis=1, keepdims=True)
        for s in s_list[1:]:
            m_cur = jnp.maximum(m_cur, jnp.max(s, axis=1, keepdims=True))
        m_new = jnp.maximum(m_prev, m_cur)
        alpha = jnp.exp(m_prev - m_new)
        m_wide = jnp.concatenate([m_new] * (page // LANES), axis=1)
        l_add = None
        pv = None
        for s, vb in zip(s_list, v_list):
            pr = jnp.exp(s - m_wide)
            row_sum = jnp.sum(pr, axis=1, keepdims=True)
            part = _dot(pr.astype(BF16), vb)
            l_add = row_sum if l_add is None else l_add + row_sum
            pv = part if pv is None else pv + part
        l_ref[...] = alpha * l_ref[...] + l_add
        acc_ref[...] = jnp.concatenate([alpha] * (D_MODEL // LANES), axis=1) * acc_ref[...] + pv
        m_ref[...] = m_new

    @pl.when(j == 0)
    def _init():
        rep = jnp.concatenate([q_ref[...]] * (n_rows // SAMPLE_PAD_T), axis=0)
        r_grp = lax.broadcasted_iota(jnp.int32, rep.shape, 0) // SAMPLE_PAD_T
        l_grp = lax.broadcasted_iota(jnp.int32, rep.shape, 1) // D_HEAD_B
        qbd_ref[...] = jnp.where(r_grp == l_grp, rep, 0.0).astype(BF16)
        m_ref[...] = jnp.full_like(m_ref, -jnp.inf)
        l_ref[...] = jnp.zeros_like(l_ref)
        acc_ref[...] = jnp.zeros_like(acc_ref)
        pad = jnp.zeros((page - SAMPLE_PAD_T, D_MODEL), F32)
        kpad_ref[...] = jnp.concatenate([kn_ref[...], pad], axis=0).astype(BF16)
        vpad_ref[...] = jnp.concatenate([vn_ref[...], pad], axis=0).astype(BF16)
        s = _dot_nt(qbd_ref[...], kpad_ref[...])
        t_q = lax.broadcasted_iota(jnp.int32, s.shape, 0) % SAMPLE_PAD_T
        t_k = lax.broadcasted_iota(jnp.int32, s.shape, 1)
        s = jnp.where((t_k <= t_q) & (t_k < t_real), s, -jnp.inf)
        update([s], [vpad_ref[...]])

    qbd = qbd_ref[...]
    update([_dot_nt(qbd, k_refs[p_i][0].astype(BF16)) for p_i in range(pp)],
           [v_refs[p_i][0].astype(BF16) for p_i in range(pp)])

    @pl.when(j == pl.num_programs(1) - 1)
    def _finish():
        lam = _lambda_value(lam_ref, lam_init)
        o = acc_ref[...] / jnp.concatenate([l_ref[...]] * (D_MODEL // LANES), axis=1)
        for h in range(N_HEADS):
            r1 = 2 * h * SAMPLE_PAD_T
            r2 = r1 + SAMPLE_PAD_T
            hs = slice(h * HEAD_DIM, (h + 1) * HEAD_DIM)
            o_ref[:, hs] = o[r1:r1 + SAMPLE_PAD_T, hs] - lam * o[r2:r2 + SAMPLE_PAD_T, hs]


def _attn_sample(q, k_new, v_new, cache_k, cache_v, page_table, lam_p, lam_init, *, t_real, pages_per_step=4):
    m = q.shape[0]
    n_seq, n_pages = page_table.shape
    page = cache_k.shape[1]
    pp = pages_per_step
    while n_pages % pp:
        pp //= 2
    lam_pad = jnp.zeros((SUBLANES, LANES), F32).at[:4, :D_HEAD_B].set(lam_p)
    n_rows = 2 * N_HEADS * SAMPLE_PAD_T
    seq = lambda s, j, pt: (s, 0)

    def page_spec(p_i):
        return pl.BlockSpec((1, page, D_MODEL), lambda s, j, pt: (pt[s * n_pages + j * pp + p_i], 0, 0))

    grid_spec = pltpu.PrefetchScalarGridSpec(
        num_scalar_prefetch=1,
        grid=(n_seq, n_pages // pp),
        in_specs=[pl.BlockSpec((SUBLANES, LANES), lambda s, j, pt: (0, 0)),
                  pl.BlockSpec((SAMPLE_PAD_T, D_MODEL), seq), pl.BlockSpec((SAMPLE_PAD_T, D_MODEL), seq),
                  pl.BlockSpec((SAMPLE_PAD_T, D_MODEL), seq)]
                 + [page_spec(p_i) for p_i in range(pp)] * 2,
        out_specs=pl.BlockSpec((SAMPLE_PAD_T, D_MODEL), seq),
        scratch_shapes=[pltpu.VMEM((n_rows, D_MODEL), BF16), pltpu.VMEM((page, D_MODEL), BF16),
                        pltpu.VMEM((page, D_MODEL), BF16), pltpu.VMEM((n_rows, LANES), F32),
                        pltpu.VMEM((n_rows, LANES), F32), pltpu.VMEM((n_rows, D_MODEL), F32)],
    )
    assert page % LANES == 0
    return pl.pallas_call(
        functools.partial(_attn_sample_kernel, pages_per_step=pp, t_real=t_real, lam_init=lam_init),
        grid_spec=grid_spec,
        out_shape=jax.ShapeDtypeStruct((m, D_MODEL), F32),
        compiler_params=_cparams(("parallel", "arbitrary")),
        name="attn_sample",
    )(page_table.reshape(-1), lam_pad, q, k_new, v_new, *([cache_k] * pp), *([cache_v] * pp))


def _rope_tables(pos):
    half = D_HEAD_B // 2
    inv_freq = ROPE_THETA ** (-jnp.arange(half, dtype=F32) / half)
    ang = pos.astype(F32)[:, None] * inv_freq[None, :]
    cos, sin = jnp.cos(ang), jnp.sin(ang)
    cos128 = jnp.tile(cos, (1, LANES // half))
    sin128 = jnp.tile(jnp.concatenate([-sin, sin], axis=1), (1, LANES // D_HEAD_B))
    return cos128, sin128


def _trunk(x, pos, conv_state, delta_state, past, wts, *, n_seq, seq_len, t_real):
    h = x
    hb = x.astype(BF16)
    cos128, sin128 = _rope_tables(pos)
    new_convs, new_states = [], []
    kf = vf = kb = vb = None
    for l in range(DEPTH):
        if l < DEPTH // 2:
            qkv_pre, z, ab = _linear(hb, wts["w_in"][l], (QKV_DIM, D_MODEL, LANES), (F32, BF16, F32))
            pre3 = qkv_pre.reshape(n_seq, seq_len, QKV_DIM)
            new_convs.append(pre3[:, t_real - (CONV_W - 1):t_real])
            if conv_state is None:
                prev = None
            else:
                prev = jnp.pad(conv_state[l], ((0, 0), (SAMPLE_PAD_T - (CONV_W - 1), 0), (0, 0)))
                prev = prev.reshape(n_seq * SAMPLE_PAD_T, QKV_DIM)
            q, k, v, gb = _conv_gate(qkv_pre, prev, ab, wts["w_conv"][l], wts["a_log"][l], wts["dt_bias"][l],
                                     seq_len=seq_len, t_real=t_real, chunk=DELTA_CHUNK)
            s0 = None if delta_state is None else delta_state[l]
            o, s_new = _delta_rule(q, k, v, gb, s0, n_seq=n_seq, seq_len=seq_len, chunk=DELTA_CHUNK)
            new_states.append(s_new)
            h, hb = _mix_out(o, z, h, wts["g_norm"][l], wts["w_out"][l], wts["ln_mix_g"][l], wts["ln_mix_b"][l])
        else:
            j = l - DEPTH // 2
            lam_init = 0.8 - 0.6 * math.exp(-0.3 * l)
            if past is None:
                q = _q_proj(hb, wts["w_q"][j], cos128, sin128, BF16)
                o = _attn_prompt(q, kb, vb, wts["lambda"][j], lam_init, n_seq=n_seq, seq_len=seq_len)
            else:
                q = _q_proj(hb, wts["w_q"][j], cos128, sin128, F32)
                o = _attn_sample(q, kf, vf, past[0], past[1], past[2], wts["lambda"][j], lam_init, t_real=t_real)
            h, hb = _mix_out(o, None, h, wts["g_sub"][j], wts["w_o"][j], wts["ln_mix_g"][l], wts["ln_mix_b"][l],
                             post_scale=1.0 - lam_init)
        if l % 2 == 0:
            i = l // 2
            h, hb = _dense_ffn(hb, h, wts["w_gate_d"][i], wts["w_up_d"][i], wts["w_down_d"][i],
                               wts["ln_ffn_g"][l], wts["ln_ffn_b"][l])
        else:
            i = l // 2
            h, hb = _moe_ffn(h, hb, wts["w_router"][i], wts["w_gate_e"][i], wts["w_up_e"][i], wts["w_down_e"][i],
                             wts["ln_ffn_g"][l], wts["ln_ffn_b"][l])
        if l == DEPTH // 2 - 1:
            kf, vf, kb, vb = _kv_proj(hb, wts["w_kv"], cos128, sin128)
    return h, jnp.stack(new_convs), jnp.stack(new_states), kf, vf


def kernel(x_prompt, x_sample, state_delta, state_conv, cache_k, cache_v, page_table, w_in_a, w_conv_a, a_log_a, dt_bias_a, g_norm_a, w_out_a, w_kv, w_q_b, lambda_b, g_sub_b, w_o_b, ln_mix_g, ln_mix_b, ln_ffn_g, ln_ffn_b, w_gate_d, w_up_d, w_down_d, w_router, w_gate_e, w_up_e, w_down_e):
    n_a = w_in_a.shape[0]
    bp, tp, _ = x_prompt.shape
    bs, ts, _ = x_sample.shape
    assert CONV_W - 1 <= ts <= SAMPLE_PAD_T and tp % SUBLANES == 0

    w_in = jnp.pad(w_in_a, ((0, 0), (0, 0), (0, LANES - 2 * N_HEADS))).astype(BF16)
    wts = dict(
        w_in=w_in, w_conv=w_conv_a, a_log=a_log_a, dt_bias=dt_bias_a, g_norm=g_norm_a,
        w_out=w_out_a.astype(BF16), w_kv=w_kv.astype(BF16), w_q=w_q_b.astype(BF16), g_sub=g_sub_b,
        w_o=w_o_b.astype(BF16), ln_mix_g=ln_mix_g, ln_mix_b=ln_mix_b, ln_ffn_g=ln_ffn_g, ln_ffn_b=ln_ffn_b,
        w_gate_d=w_gate_d.astype(BF16), w_up_d=w_up_d.astype(BF16), w_down_d=w_down_d.astype(BF16),
        w_router=w_router, w_gate_e=w_gate_e.astype(BF16), w_up_e=w_up_e.astype(BF16),
        w_down_e=w_down_e.astype(BF16))
    wts["lambda"] = lambda_b

    pos_p = jnp.tile(jnp.arange(tp), bp)
    y_p, conv_p, delta_p, k_p, v_p = _trunk(x_prompt.reshape(bp * tp, D_MODEL), pos_p, None, None, None, wts,
                                            n_seq=bp, seq_len=tp, t_real=tp)

    n_pages = page_table.shape[1]
    page = cache_k.shape[1]
    past_len = n_pages * page
    xs = jnp.pad(x_sample, ((0, 0), (0, SAMPLE_PAD_T - ts), (0, 0))).reshape(bs * SAMPLE_PAD_T, D_MODEL)
    pos_s = jnp.tile(past_len + jnp.arange(SAMPLE_PAD_T), bs)
    past = (cache_k.reshape(cache_k.shape[0], page, D_MODEL), cache_v.reshape(cache_v.shape[0], page, D_MODEL),
            page_table)
    y_s, conv_s, delta_s, k_s, v_s = _trunk(xs, pos_s, state_conv, state_delta, past, wts,
                                            n_seq=bs, seq_len=SAMPLE_PAD_T, t_real=ts)

    def unpad(a):
        return a.reshape(bs, SAMPLE_PAD_T, D_MODEL)[:, :ts]

    return (y_p.reshape(bp, tp, D_MODEL), unpad(y_s), delta_p, conv_p,
            k_p.reshape(bp, tp, N_HEADS, 2, D_HEAD_B), v_p.reshape(bp, tp, N_HEADS, HEAD_DIM),
            delta_s, conv_s,
            unpad(k_s).reshape(bs, ts, N_HEADS, 2, D_HEAD_B), unpad(v_s).reshape(bs, ts, N_HEADS, HEAD_DIM))
```

```python
import functools
import math

import jax
import jax.numpy as jnp
from jax import lax
from jax.experimental import pallas as pl
from jax.experimental.pallas import tpu as pltpu

F32 = jnp.float32
BF16 = jnp.bfloat16
HIGHEST = lax.Precision.HIGHEST

LANES = 128
SUBLANES = 8
BF16_ROWS = 16
VMEM_LIMIT = 56 * 1024 * 1024

D_MODEL = 1024
N_HEADS = 8
HEAD_DIM = 128
QKV_DIM = 3 * D_MODEL
CONV_W = 4
DELTA_CHUNK = 64
D_HEAD_B = 64
ROPE_THETA = 10000.0
N_EXPERTS = 8
DEPTH = 4
DEEPNORM_ALPHA = (2.0 * DEPTH) ** 0.25
LN_EPS = 1e-5
RMS_EPS = 1e-6
SAMPLE_PAD_T = 8


def _cparams(semantics):
    return pltpu.CompilerParams(dimension_semantics=semantics, vmem_limit_bytes=VMEM_LIMIT)


def _dot(a, b, precision=None):
    return jnp.dot(a, b, preferred_element_type=F32, precision=precision)


def _dot_nt(a, b, precision=None):
    return lax.dot_general(a, b, (((1,), (1,)), ((), ())), preferred_element_type=F32, precision=precision)


def _dot_tn(a, b, precision=None):
    return lax.dot_general(a, b, (((0,), (0,)), ((), ())), preferred_element_type=F32, precision=precision)


def _layer_norm(y, g, b):
    mu = jnp.mean(y, axis=-1, keepdims=True)
    d = y - mu
    var = jnp.mean(d * d, axis=-1, keepdims=True)
    return d * lax.rsqrt(var + LN_EPS) * g + b


def _silu(x):
    return x * jax.nn.sigmoid(x)


def _row_tile(m, pref):
    t = min(pref, m)
    while m % t:
        t //= 2
    return t


def _linear_kernel(x_ref, w_ref, *out_refs, widths, tn):
    x = x_ref[...]
    col = 0
    for o_ref, width in zip(out_refs, widths):
        for c in range(0, width, tn):
            cw = min(tn, width - c)
            o_ref[:, c:c + cw] = _dot(x, w_ref[:, col + c:col + c + cw]).astype(o_ref.dtype)
        col += width


def _linear(x, w, widths, dtypes, tm=256, tn=512):
    m, k = x.shape
    tm = _row_tile(m, tm)
    n = sum(widths)
    return pl.pallas_call(
        functools.partial(_linear_kernel, widths=tuple(widths), tn=tn),
        grid=(m // tm,),
        in_specs=[pl.BlockSpec((tm, k), lambda i: (i, 0)),
                  pl.BlockSpec((k, n), lambda i: (0, 0))],
        out_specs=[pl.BlockSpec((tm, wd), lambda i: (i, 0)) for wd in widths],
        out_shape=[jax.ShapeDtypeStruct((m, wd), dt) for wd, dt in zip(widths, dtypes)],
        compiler_params=_cparams(("parallel",)),
        name="linear",
    )(x, w)


def _rope(x, cos128, sin128):
    n = x.shape[1]
    reps = n // LANES
    cos_t = jnp.concatenate([cos128] * reps, axis=1)
    sin_t = jnp.concatenate([sin128] * reps, axis=1)
    lane = lax.broadcasted_iota(jnp.int32, x.shape, 1)
    first_half = (lane % D_HEAD_B) < (D_HEAD_B // 2)
    rot = jnp.where(first_half, pltpu.roll(x, n - D_HEAD_B // 2, 1), pltpu.roll(x, D_HEAD_B // 2, 1))
    return x * cos_t + rot * sin_t


def _q_proj_kernel(x_ref, w_ref, cos_ref, sin_ref, q_ref, *, scale):
    q = _dot(x_ref[...], w_ref[...])
    q_ref[...] = (_rope(q, cos_ref[...], sin_ref[...]) * scale).astype(q_ref.dtype)


def _q_proj(x, w, cos, sin, out_dtype, tm=256):
    m, k = x.shape
    tm = _row_tile(m, tm)
    return pl.pallas_call(
        functools.partial(_q_proj_kernel, scale=D_HEAD_B ** -0.5 * math.log2(math.e)),
        grid=(m // tm,),
        in_specs=[pl.BlockSpec((tm, k), lambda i: (i, 0)),
                  pl.BlockSpec((k, D_MODEL), lambda i: (0, 0)),
                  pl.BlockSpec((tm, LANES), lambda i: (i, 0)),
                  pl.BlockSpec((tm, LANES), lambda i: (i, 0))],
        out_specs=pl.BlockSpec((tm, D_MODEL), lambda i: (i, 0)),
        out_shape=jax.ShapeDtypeStruct((m, D_MODEL), out_dtype),
        compiler_params=_cparams(("parallel",)),
        name="q_proj",
    )(x, w, cos, sin)


def _kv_proj_kernel(x_ref, w_ref, cos_ref, sin_ref, k_ref, v_ref, kb_ref, vb_ref):
    x = x_ref[...]
    k = _rope(_dot(x, w_ref[:, :D_MODEL]), cos_ref[...], sin_ref[...])
    v = _dot(x, w_ref[:, D_MODEL:])
    k_ref[...] = k
    v_ref[...] = v
    kb_ref[...] = k.astype(BF16)
    vb_ref[...] = v.astype(BF16)


def _kv_proj(x, w, cos, sin, tm=256):
    m, k = x.shape
    tm = _row_tile(m, tm)
    row = lambda i: (i, 0)
    return pl.pallas_call(
        _kv_proj_kernel,
        grid=(m // tm,),
        in_specs=[pl.BlockSpec((tm, k), row),
                  pl.BlockSpec((k, 2 * D_MODEL), lambda i: (0, 0)),
                  pl.BlockSpec((tm, LANES), row),
                  pl.BlockSpec((tm, LANES), row)],
        out_specs=[pl.BlockSpec((tm, D_MODEL), row)] * 4,
        out_shape=[jax.ShapeDtypeStruct((m, D_MODEL), F32)] * 2 + [jax.ShapeDtypeStruct((m, D_MODEL), BF16)] * 2,
        compiler_params=_cparams(("parallel",)),
        name="kv_proj",
    )(x, w, cos, sin)


def _conv_gate_kernel(x_ref, p_ref, ab_ref, cw_ref, alog_ref, dtb_ref, q_ref, k_ref, v_ref, gb_ref,
                      *, seg, t_real, chunk, blocks_per_seq):
    tm = x_ref.shape[0]
    row = lax.broadcasted_iota(jnp.int32, (tm, 1), 0)
    tloc = row % seg if seg < tm else row
    valid = tloc < t_real
    if seg == tm:
        not_first = (pl.program_id(0) % blocks_per_seq) != 0
        row8 = lax.broadcasted_iota(jnp.int32, (SUBLANES, 1), 0)

    for c in range(QKV_DIM // LANES):
        cs = slice(c * LANES, (c + 1) * LANES)
        xc = x_ref[:, cs]
        acc = cw_ref[CONV_W - 1:CONV_W, cs] * xc
        for j in range(1, CONV_W):
            sh = pltpu.roll(xc, j, 0)
            if seg == tm:
                halo = jnp.where(not_first, p_ref[:, cs], 0.0)
                head = jnp.where(row8 < j, pltpu.roll(halo, j, 0), sh[:SUBLANES])
                sh = jnp.concatenate([head, sh[SUBLANES:]], axis=0)
            else:
                prev = pltpu.roll(p_ref[:, cs], tm + j - SUBLANES, 0)
                sh = jnp.where(tloc >= j, sh, prev)
            acc = acc + cw_ref[CONV_W - 1 - j:CONV_W - j, cs] * sh
        y = _silu(acc)
        if c < 2 * N_HEADS:
            y = y * lax.rsqrt(jnp.sum(y * y, axis=-1, keepdims=True) + 1e-6)
        if c < N_HEADS:
            y = y * (HEAD_DIM ** -0.5)
        y = jnp.where(valid, y, 0.0)
        if c < N_HEADS:
            q_ref[:, cs] = y
        elif c < 2 * N_HEADS:
            k_ref[:, (c - N_HEADS) * LANES:(c - N_HEADS + 1) * LANES] = y
        else:
            v_ref[:, (c - 2 * N_HEADS) * LANES:(c - 2 * N_HEADS + 1) * LANES] = y

    ab = ab_ref[...]
    z = ab + dtb_ref[...]
    softplus = jnp.maximum(z, 0.0) + jnp.log1p(jnp.exp(-jnp.abs(z)))
    g = jnp.where(valid, -jnp.exp(alog_ref[...]) * softplus, 0.0)
    beta = jnp.where(valid, jax.nn.sigmoid(ab), 0.0)
    ri = lax.broadcasted_iota(jnp.int32, (tm, tm), 0)
    ci = lax.broadcasted_iota(jnp.int32, (tm, tm), 1)
    tril = jnp.where((ri // chunk == ci // chunk) & (ci <= ri), 1.0, 0.0)
    g_cum = _dot(tril, g, precision=HIGHEST)
    lane = lax.broadcasted_iota(jnp.int32, (tm, LANES), 1)
    gb_ref[...] = jnp.where(lane < N_HEADS, g_cum, beta)


def _conv_gate(qkv_pre, prev, ab, conv_w, a_log, dt_bias, *, seq_len, t_real, chunk, tm=256):
    m = qkv_pre.shape[0]
    if prev is None:
        tm = _row_tile(seq_len, tm)
        seg = tm
        blocks_per_seq = seq_len // tm
        tiles_per_block = tm // SUBLANES
        p_arr = qkv_pre
        p_spec = pl.BlockSpec((SUBLANES, QKV_DIM), lambda i: (jnp.maximum(i * tiles_per_block - 1, 0), 0))
    else:
        tm = _row_tile(m, tm)
        seg = seq_len
        blocks_per_seq = 1
        p_arr = prev
        p_spec = pl.BlockSpec((tm, QKV_DIM), lambda i: (i, 0))
    assert seg % chunk == 0 or chunk % seg == 0
    row = lambda i: (i, 0)
    const = lambda i: (0, 0)
    cw = jnp.zeros((SUBLANES, QKV_DIM), F32).at[:CONV_W].set(conv_w)
    alog = jnp.zeros((1, LANES), F32).at[0, :N_HEADS].set(a_log)
    dtb = jnp.zeros((1, LANES), F32).at[0, :N_HEADS].set(dt_bias)
    return pl.pallas_call(
        functools.partial(_conv_gate_kernel, seg=seg, t_real=t_real, chunk=min(chunk, seg),
                          blocks_per_seq=blocks_per_seq),
        grid=(m // tm,),
        in_specs=[pl.BlockSpec((tm, QKV_DIM), row), p_spec, pl.BlockSpec((tm, LANES), row),
                  pl.BlockSpec((SUBLANES, QKV_DIM), const), pl.BlockSpec((1, LANES), const),
                  pl.BlockSpec((1, LANES), const)],
        out_specs=[pl.BlockSpec((tm, D_MODEL), row)] * 3 + [pl.BlockSpec((tm, LANES), row)],
        out_shape=[jax.ShapeDtypeStruct((m, D_MODEL), F32)] * 3 + [jax.ShapeDtypeStruct((m, LANES), F32)],
        compiler_params=_cparams(("parallel",)),
        name="conv_gate",
    )(qkv_pre, p_arr, ab, cw, alog, dtb)


def _delta_prepare(qs, ks, vs, gs, bs):
    n = len(qs)
    c = qs[0].shape[0]
    ri = lax.broadcasted_iota(jnp.int32, (c, c), 0)
    ci = lax.broadcasted_iota(jnp.int32, (c, c), 1)
    causal = ci <= ri
    strict = ci < ri
    eye = ci == ri
    decay, kb, k_b, pw = [], [], [], []
    for i in range(n):
        g_row = jnp.sum(jnp.where(eye, gs[i], 0.0), axis=0, keepdims=True)
        decay.append(jnp.where(causal, jnp.exp(jnp.where(causal, gs[i] - g_row, 0.0)), 0.0))
        kb.append(ks[i] * bs[i])
        k_b.append(ks[i].astype(BF16))
    for i in range(n):
        pw.append(-jnp.where(strict, _dot_nt(kb[i].astype(BF16), k_b[i]) * decay[i], 0.0))
    inv_m1 = list(pw)
    for _ in range(int(math.log2(c)) - 1):
        for i in range(n):
            pw_b = pw[i].astype(BF16)
            pw[i] = _dot(pw_b, pw_b)
        for i in range(n):
            inv_m1[i] = inv_m1[i] + pw[i] + _dot(inv_m1[i].astype(BF16), pw[i].astype(BF16))
    out = []
    for i in range(n):
        e_g = jnp.exp(gs[i])
        rhs = jnp.concatenate([vs[i] * bs[i], kb[i] * e_g], axis=1)
        sol = rhs + _dot(inv_m1[i].astype(BF16), rhs.astype(BF16))
        attn = _dot_nt(qs[i].astype(BF16), k_b[i]) * decay[i]
        g_last = gs[i][c - 1:c, :]
        out.append(dict(u=sol[:, :HEAD_DIM], w=sol[:, HEAD_DIM:].astype(BF16), attn=attn.astype(BF16),
                        q_dec=(qs[i] * e_g).astype(BF16),
                        k_dec=(ks[i] * jnp.exp(g_last - gs[i])).astype(BF16), decay_last=jnp.exp(g_last)))
    return out


def _delta_apply(prep, states):
    n = len(prep)
    sb = [s.astype(BF16) for s in states]
    v_new = [(prep[i]["u"] - _dot(prep[i]["w"], sb[i])).astype(BF16) for i in range(n)]
    outs = [_dot(prep[i]["q_dec"], sb[i]) + _dot(prep[i]["attn"], v_new[i]) for i in range(n)]
    new_states = [states[i] * prep[i]["decay_last"] + _dot_tn(prep[i]["k_dec"], v_new[i]) for i in range(n)]
    return outs, new_states


def _delta_kernel(*refs, chunk, has_s0):
    if has_s0:
        q_ref, k_ref, v_ref, gb_ref, s0_ref, o_ref, s_ref = refs
    else:
        q_ref, k_ref, v_ref, gb_ref, o_ref, s_ref = refs
        s0_ref = None
    rows = q_ref.shape[0]

    @pl.when(pl.program_id(1) == 0)
    def _init():
        if has_s0:
            s_ref[...] = s0_ref[...]
        else:
            s_ref[...] = jnp.zeros_like(s_ref)

    n_chunks = rows // chunk
    qs, ks, vs, gs, bs = [], [], [], [], []
    for ci in range(n_chunks):
        rs = slice(ci * chunk, (ci + 1) * chunk)
        gb = gb_ref[rs, :]
        for h in range(N_HEADS):
            hs = slice(h * HEAD_DIM, (h + 1) * HEAD_DIM)
            qs.append(q_ref[rs, hs])
            ks.append(k_ref[rs, hs])
            vs.append(v_ref[rs, hs])
            gs.append(gb[:, h:h + 1])
            bs.append(gb[:, N_HEADS + h:N_HEADS + h + 1])
    prep = _delta_prepare(qs, ks, vs, gs, bs)
    states = [s_ref[0, h] for h in range(N_HEADS)]
    for ci in range(n_chunks):
        outs, states = _delta_apply(prep[ci * N_HEADS:(ci + 1) * N_HEADS], states)
        for h in range(N_HEADS):
            o_ref[ci * chunk:(ci + 1) * chunk, h * HEAD_DIM:(h + 1) * HEAD_DIM] = outs[h]
    for h in range(N_HEADS):
        s_ref[0, h] = states[h]


def _delta_rule(q, k, v, gb, s0, *, n_seq, seq_len, chunk, rows_per_step=128):
    m = q.shape[0]
    chunk = min(chunk, seq_len)
    rows = _row_tile(seq_len, max(rows_per_step, chunk))
    steps = seq_len // rows
    row = lambda s, i: (s * steps + i, 0)
    st = lambda s, i: (s, 0, 0, 0)
    in_specs = [pl.BlockSpec((rows, D_MODEL), row)] * 3 + [pl.BlockSpec((rows, LANES), row)]
    args = [q, k, v, gb]
    if s0 is not None:
        in_specs.append(pl.BlockSpec((1, N_HEADS, HEAD_DIM, HEAD_DIM), st))
        args.append(s0)
    return pl.pallas_call(
        functools.partial(_delta_kernel, chunk=chunk, has_s0=s0 is not None),
        grid=(n_seq, steps),
        in_specs=in_specs,
        out_specs=[pl.BlockSpec((rows, D_MODEL), row), pl.BlockSpec((1, N_HEADS, HEAD_DIM, HEAD_DIM), st)],
        out_shape=[jax.ShapeDtypeStruct((m, D_MODEL), F32),
                   jax.ShapeDtypeStruct((n_seq, N_HEADS, HEAD_DIM, HEAD_DIM), F32)],
        compiler_params=_cparams(("parallel", "arbitrary")),
        name="delta_rule",
    )(*args)


def _mix_out_kernel(*refs, gated, post_scale):
    if gated:
        o_ref, z_ref, h_ref, gn_ref, w_ref, lg_ref, lb_ref, hf_ref, hb_ref = refs
    else:
        o_ref, h_ref, gn_ref, w_ref, lg_ref, lb_ref, hf_ref, hb_ref = refs
    parts = []
    for hd in range(N_HEADS):
        hs = slice(hd * HEAD_DIM, (hd + 1) * HEAD_DIM)
        o = o_ref[:, hs]
        y = o * lax.rsqrt(jnp.mean(o * o, axis=-1, keepdims=True) + RMS_EPS) * gn_ref[:, hs]
        if gated:
            y = y * _silu(z_ref[:, hs].astype(F32))
        else:
            y = y * post_scale
        parts.append(y.astype(BF16))
    mix = _dot(jnp.concatenate(parts, axis=1), w_ref[...])
    hn = _layer_norm(DEEPNORM_ALPHA * h_ref[...] + mix, lg_ref[...], lb_ref[...])
    hf_ref[...] = hn
    hb_ref[...] = hn.astype(BF16)


def _mix_out(o, z, h, g_norm, w, ln_g, ln_b, *, post_scale=1.0, tm=256):
    m = o.shape[0]
    tm = _row_tile(m, tm)
    row = lambda i: (i, 0)
    const = lambda i: (0, 0)
    gated = z is not None
    gn = jnp.tile(g_norm.reshape(1, HEAD_DIM), (1, N_HEADS))
    args = [o] + ([z] if gated else []) + [h, gn, w, ln_g.reshape(1, -1), ln_b.reshape(1, -1)]
    in_specs = ([pl.BlockSpec((tm, D_MODEL), row)] * (3 if gated else 2)
                + [pl.BlockSpec((1, D_MODEL), const), pl.BlockSpec((D_MODEL, D_MODEL), const),
                   pl.BlockSpec((1, D_MODEL), const), pl.BlockSpec((1, D_MODEL), const)])
    return pl.pallas_call(
        functools.partial(_mix_out_kernel, gated=gated, post_scale=post_scale),
        grid=(m // tm,),
        in_specs=in_specs,
        out_specs=[pl.BlockSpec((tm, D_MODEL), row)] * 2,
        out_shape=[jax.ShapeDtypeStruct((m, D_MODEL), F32), jax.ShapeDtypeStruct((m, D_MODEL), BF16)],
        compiler_params=_cparams(("parallel",)),
        name="mix_out",
    )(*args)


def _dense_ffn_kernel(xb_ref, h_ref, wg_ref, wu_ref, wd_ref, lg_ref, lb_ref, hf_ref, hb_ref, *, tf):
    x = xb_ref[...]
    f_dim = wg_ref.shape[1]
    acc = DEEPNORM_ALPHA * h_ref[...]
    for c in range(0, f_dim, tf):
        cw = min(tf, f_dim - c)
        g = _dot(x, wg_ref[:, c:c + cw])
        u = _dot(x, wu_ref[:, c:c + cw])
        acc = acc + _dot((_silu(g) * u).astype(BF16), wd_ref[c:c + cw, :])
    hn = _layer_norm(acc, lg_ref[...], lb_ref[...])
    hf_ref[...] = hn
    hb_ref[...] = hn.astype(BF16)


def _dense_ffn(xb, h, wg, wu, wd, ln_g, ln_b, *, tm=256, tf=256):
    m = xb.shape[0]
    f_dim = wg.shape[1]
    tm = _row_tile(m, tm)
    row = lambda i: (i, 0)
    const = lambda i: (0, 0)
    return pl.pallas_call(
        functools.partial(_dense_ffn_kernel, tf=tf),
        grid=(m // tm,),
        in_specs=[pl.BlockSpec((tm, D_MODEL), row), pl.BlockSpec((tm, D_MODEL), row),
                  pl.BlockSpec((D_MODEL, f_dim), const), pl.BlockSpec((D_MODEL, f_dim), const),
                  pl.BlockSpec((f_dim, D_MODEL), const),
                  pl.BlockSpec((1, D_MODEL), const), pl.BlockSpec((1, D_MODEL), const)],
        out_specs=[pl.BlockSpec((tm, D_MODEL), row)] * 2,
        out_shape=[jax.ShapeDtypeStruct((m, D_MODEL), F32), jax.ShapeDtypeStruct((m, D_MODEL), BF16)],
        compiler_params=_cparams(("parallel",)),
        name="dense_ffn",
    )(xb, h, wg, wu, wd, ln_g.reshape(1, -1), ln_b.reshape(1, -1))


MOE_TM = 256
MOE_BM = 512
MOE_ALIGN = BF16_ROWS


def _router_kernel(h_ref, wr_ref, a_ref, r_ref, g_ref, cnt_ref):
    tm = h_ref.shape[0]
    logits = _dot_nt(wr_ref[...], h_ref[...], precision=HIGHEST)
    e_idx = lax.broadcasted_iota(jnp.int32, logits.shape, 0).astype(F32)
    m1 = jnp.max(logits, axis=0, keepdims=True)
    i1 = jnp.min(jnp.where(logits == m1, e_idx, float(N_EXPERTS)), axis=0, keepdims=True)
    first = e_idx == i1
    rest = jnp.where(first, -jnp.inf, logits)
    m2 = jnp.max(rest, axis=0, keepdims=True)
    i2 = jnp.min(jnp.where(rest == m2, e_idx, float(N_EXPERTS)), axis=0, keepdims=True)
    second = e_idx == i2
    ex = jnp.exp(m2 - m1)
    g1 = 1.0 / (1.0 + ex)
    g2 = ex / (1.0 + ex)
    routed = jnp.where(first | second, 1.0, 0.0)
    ji = lax.broadcasted_iota(jnp.int32, (tm, tm), 0)
    ii = lax.broadcasted_iota(jnp.int32, (tm, tm), 1)
    before = jnp.where(ji < ii, 1.0, 0.0).astype(BF16)
    a_ref[...] = routed
    r_ref[...] = _dot(routed.astype(BF16), before)
    g_ref[...] = jnp.where(first, g1, 0.0) + jnp.where(second, g2, 0.0)
    cnt_ref[0] = jnp.broadcast_to(jnp.sum(routed, axis=1, keepdims=True), (N_EXPERTS, LANES))


def _router(h, w_router_t, tm):
    m = h.shape[0]
    nb = m // tm
    col = lambda i: (0, i)
    return pl.pallas_call(
        _router_kernel,
        grid=(nb,),
        in_specs=[pl.BlockSpec((tm, D_MODEL), lambda i: (i, 0)),
                  pl.BlockSpec((N_EXPERTS, D_MODEL), lambda i: (0, 0))],
        out_specs=[pl.BlockSpec((N_EXPERTS, tm), col)] * 3
                  + [pl.BlockSpec((1, N_EXPERTS, LANES), lambda i: (i, 0, 0))],
        out_shape=[jax.ShapeDtypeStruct((N_EXPERTS, m), F32)] * 3
                  + [jax.ShapeDtypeStruct((nb, N_EXPERTS, LANES), F32)],
        compiler_params=_cparams(("parallel",)),
        name="moe_router",
    )(h, w_router_t)


def _dispatch_kernel(off_ref, xb_ref, a_ref, r_ref, rows_in_ref, rows_ref, slab_ref, sem_ref):
    del rows_in_ref
    b = pl.program_id(0)
    nb = pl.num_programs(0)
    slot = b % 2
    tm = xb_ref.shape[0]

    def slab_copy(which_slot, which_block, e):
        off = pl.multiple_of(off_ref[which_block * N_EXPERTS + e], MOE_ALIGN)
        return pltpu.make_async_copy(slab_ref.at[which_slot, e], rows_ref.at[pl.ds(off, tm)],
                                     sem_ref.at[which_slot, e])

    x = xb_ref[...]
    r_idx = lax.broadcasted_iota(jnp.int32, (tm, tm), 0).astype(F32)
    for e in range(N_EXPERTS):
        pick = jnp.where((r_idx == r_ref[e:e + 1, :]) & (a_ref[e:e + 1, :] > 0.0), 1.0, 0.0)
        slab_ref[slot, e] = _dot(pick.astype(BF16), x).astype(BF16)

    @pl.when(b > 0)
    def _previous_block_landed():
        for e in range(N_EXPERTS):
            slab_copy(1 - slot, b - 1, e).wait()

    for e in range(N_EXPERTS):
        slab_copy(slot, b, e).start()

    @pl.when(b == nb - 1)
    def _drain():
        for e in range(N_EXPERTS):
            slab_copy(slot, b, e).wait()


def _dispatch(xb, a_t, r_t, off, n_rows, tm):
    m = xb.shape[0]
    nb = m // tm
    rows_init = jnp.zeros((n_rows, D_MODEL), BF16)
    grid_spec = pltpu.PrefetchScalarGridSpec(
        num_scalar_prefetch=1,
        grid=(nb,),
        in_specs=[pl.BlockSpec((tm, D_MODEL), lambda b, off: (b, 0)),
                  pl.BlockSpec((N_EXPERTS, tm), lambda b, off: (0, b)),
                  pl.BlockSpec((N_EXPERTS, tm), lambda b, off: (0, b)),
                  pl.BlockSpec(memory_space=pl.ANY)],
        out_specs=pl.BlockSpec(memory_space=pl.ANY),
        scratch_shapes=[pltpu.VMEM((2, N_EXPERTS, tm, D_MODEL), BF16),
                        pltpu.SemaphoreType.DMA((2, N_EXPERTS))],
    )
    return pl.pallas_call(
        _dispatch_kernel,
        grid_spec=grid_spec,
        out_shape=jax.ShapeDtypeStruct((n_rows, D_MODEL), BF16),
        input_output_aliases={4: 0},
        compiler_params=_cparams(("arbitrary",)),
        name="moe_dispatch",
    )(off, xb, a_t, r_t, rows_init)


def _expert_ffn_kernel(be_ref, valid_ref, x_ref, wg_ref, wu_ref, wd_ref, y_ref, acc_ref):
    del be_ref
    g_idx = pl.program_id(0)
    f = pl.program_id(1)
    nf = pl.num_programs(1)
    is_valid = valid_ref[g_idx] > 0

    @pl.when(is_valid)
    def _compute():
        x = x_ref[...]
        g = _dot(x, wg_ref[0])
        u = _dot(x, wu_ref[0])
        part = _dot((_silu(g) * u).astype(BF16), wd_ref[0])

        @pl.when(f == 0)
        def _():
            acc_ref[...] = part

        @pl.when(f > 0)
        def _():
            acc_ref[...] += part

        @pl.when(f == nf - 1)
        def _():
            y_ref[...] = acc_ref[...].astype(y_ref.dtype)

    @pl.when(jnp.logical_not(is_valid) & (f == nf - 1))
    def _skip():
        y_ref[...] = jnp.zeros_like(y_ref)


def _expert_ffn(x_rows, blk_e, valid, wg, wu, wd, bm, tf):
    n_rows = x_rows.shape[0]
    f_dim = wg.shape[2]
    nf = f_dim // tf
    grid_spec = pltpu.PrefetchScalarGridSpec(
        num_scalar_prefetch=2,
        grid=(n_rows // bm, nf),
        in_specs=[pl.BlockSpec((bm, D_MODEL), lambda g, f, be, va: (g, 0)),
                  pl.BlockSpec((1, D_MODEL, tf), lambda g, f, be, va: (be[g], 0, f)),
                  pl.BlockSpec((1, D_MODEL, tf), lambda g, f, be, va: (be[g], 0, f)),
                  pl.BlockSpec((1, tf, D_MODEL), lambda g, f, be, va: (be[g], f, 0))],
        out_specs=pl.BlockSpec((bm, D_MODEL), lambda g, f, be, va: (g, 0)),
        scratch_shapes=[pltpu.VMEM((bm, D_MODEL), F32)],
    )
    return pl.pallas_call(
        _expert_ffn_kernel,
        grid_spec=grid_spec,
        out_shape=jax.ShapeDtypeStruct((n_rows, D_MODEL), BF16),
        compiler_params=_cparams(("parallel", "arbitrary")),
        name="moe_expert_ffn",
    )(blk_e, valid, x_rows, wg, wu, wd)


def _combine_kernel(off_ref, h_ref, a_ref, r_ref, g_ref, lg_ref, lb_ref, rows_ref, hf_ref, hb_ref,
                    slab_ref, sem_ref, acc_ref):
    b = pl.program_id(0)
    nb = pl.num_programs(0)
    slot = b % 2
    tm = h_ref.shape[0]

    def slab_copy(which_slot, which_block, e):
        off = pl.multiple_of(off_ref[which_block * N_EXPERTS + e], MOE_ALIGN)
        return pltpu.make_async_copy(rows_ref.at[pl.ds(off, tm)], slab_ref.at[which_slot, e],
                                     sem_ref.at[which_slot, e])

    @pl.when(b == 0)
    def _prime():
        for e in range(N_EXPERTS):
            slab_copy(0, 0, e).start()

    @pl.when(b + 1 < nb)
    def _prefetch():
        for e in range(N_EXPERTS):
            slab_copy(1 - slot, b + 1, e).start()

    acc_ref[...] = DEEPNORM_ALPHA * h_ref[...]
    r_idx = lax.broadcasted_iota(jnp.int32, (tm, tm), 1).astype(F32)
    for e in range(N_EXPERTS):
        pick = jnp.where((r_idx == r_ref[:, e:e + 1]) & (a_ref[:, e:e + 1] > 0.0), 1.0, 0.0)
        slab_copy(slot, b, e).wait()
        acc_ref[...] += g_ref[:, e:e + 1] * _dot(pick.astype(BF16), slab_ref[slot, e])
    hn = _layer_norm(acc_ref[...], lg_ref[...], lb_ref[...])
    hf_ref[...] = hn
    hb_ref[...] = hn.astype(BF16)


def _combine(h, a_c, r_c, g_c, ln_g, ln_b, y_rows, off, tm):
    m = h.shape[0]
    nb = m // tm
    blk = lambda b, off: (b, 0)
    const = lambda b, off: (0, 0)
    grid_spec = pltpu.PrefetchScalarGridSpec(
        num_scalar_prefetch=1,
        grid=(nb,),
        in_specs=[pl.BlockSpec((tm, D_MODEL), blk),
                  pl.BlockSpec((tm, N_EXPERTS), blk), pl.BlockSpec((tm, N_EXPERTS), blk),
                  pl.BlockSpec((tm, N_EXPERTS), blk),
                  pl.BlockSpec((1, D_MODEL), const), pl.BlockSpec((1, D_MODEL), const),
                  pl.BlockSpec(memory_space=pl.ANY)],
        out_specs=[pl.BlockSpec((tm, D_MODEL), blk)] * 2,
        scratch_shapes=[pltpu.VMEM((2, N_EXPERTS, tm, D_MODEL), BF16), pltpu.SemaphoreType.DMA((2, N_EXPERTS)),
                        pltpu.VMEM((tm, D_MODEL), F32)],
    )
    return pl.pallas_call(
        _combine_kernel,
        grid_spec=grid_spec,
        out_shape=[jax.ShapeDtypeStruct((m, D_MODEL), F32), jax.ShapeDtypeStruct((m, D_MODEL), BF16)],
        compiler_params=_cparams(("arbitrary",)),
        name="moe_combine",
    )(off, h, a_c, r_c, g_c, ln_g.reshape(1, -1), ln_b.reshape(1, -1), y_rows)


def _moe_ffn(h, hb, w_router, wg, wu, wd, ln_g, ln_b):
    m = h.shape[0]
    tm = _row_tile(m, MOE_TM)
    nb = m // tm
    bm = MOE_BM
    f_dim = wg.shape[2]
    tf = f_dim // 2 if (f_dim // 2) % 256 == 0 else f_dim
    a_t, r_t, g_t, cnts = _router(h, w_router.T, tm)

    cnt = cnts[:, :, 0].astype(jnp.int32)
    padded = (cnt + MOE_ALIGN - 1) // MOE_ALIGN * MOE_ALIGN
    base = jnp.cumsum(padded, axis=0) - padded
    used = jnp.sum(padded, axis=0)
    cap = (used + tm + bm - 1) // bm * bm
    region = jnp.cumsum(cap) - cap
    off = (region[None, :] + base).reshape(-1).astype(jnp.int32)
    max_rows = 2 * m + nb * N_EXPERTS * (MOE_ALIGN - 1) + N_EXPERTS * (tm + bm)
    n_rows = (max_rows + bm - 1) // bm * bm
    g_idx = jnp.arange(n_rows // bm, dtype=jnp.int32)
    region_blk = region // bm
    blk_e = (jnp.sum(g_idx[:, None] >= region_blk[None, :], axis=1) - 1).astype(jnp.int32)
    n_blk = (used + bm - 1) // bm
    valid = ((g_idx - region_blk[blk_e]) < n_blk[blk_e]).astype(jnp.int32)

    x_rows = _dispatch(hb, a_t, r_t, off, n_rows, tm)
    y_rows = _expert_ffn(x_rows, blk_e, valid, wg, wu, wd, bm, tf)
    return _combine(h, a_t.T, r_t.T, g_t.T, ln_g, ln_b, y_rows, off, tm)


def _lambda_value(lam_ref, lam_init):
    s1 = jnp.sum(lam_ref[0:1, :] * lam_ref[1:2, :], axis=1, keepdims=True)
    s2 = jnp.sum(lam_ref[2:3, :] * lam_ref[3:4, :], axis=1, keepdims=True)
    return jnp.exp(s1) - jnp.exp(s2) + lam_init


def _attn_prompt_kernel(qi_ref, kj_ref, lam_ref, q_ref, k_ref, v_ref, o_ref, qq_ref, m_ref, acc_ref,
                        *, lam_init):
    p = pl.program_id(2)
    i = qi_ref[p]
    j = kj_ref[p]
    tq = q_ref.shape[0]
    tk = k_ref.shape[0]

    @pl.when(j == 0)
    def _init():
        q = q_ref[...]
        lane = lax.broadcasted_iota(jnp.int32, q.shape, 1)
        zero = jnp.zeros_like(q)
        qq_ref[0:tq, :] = jnp.where(lane < D_HEAD_B, q, zero)
        qq_ref[tq:2 * tq, :] = jnp.where(lane >= D_HEAD_B, q, zero)
        m_ref[...] = jnp.full_like(m_ref, -jnp.inf)
        acc_ref[...] = jnp.zeros_like(acc_ref)

    def step(masked):
        s = _dot_nt(qq_ref[...], k_ref[...])
        if masked:
            qpos = lax.broadcasted_iota(jnp.int32, s.shape, 0) % tq
            kpos = lax.broadcasted_iota(jnp.int32, s.shape, 1)
            s = jnp.where(kpos <= qpos, s, -jnp.inf)
        m_prev = m_ref[...]
        m_new = jnp.maximum(m_prev, jnp.max(s, axis=1, keepdims=True))
        alpha = jnp.exp2(m_prev - m_new)
        pr = jnp.exp2(s - jnp.concatenate([m_new] * (tk // LANES), axis=1))
        acc_ref[:, HEAD_DIM:] = alpha * acc_ref[:, HEAD_DIM:] + jnp.sum(pr, axis=1, keepdims=True)
        acc_ref[:, :HEAD_DIM] = alpha * acc_ref[:, :HEAD_DIM] + _dot(pr.astype(BF16), v_ref[...])
        m_ref[...] = m_new

    @pl.when(j < i)
    def _full():
        step(False)

    @pl.when(j == i)
    def _diag():
        step(True)
        lam = _lambda_value(lam_ref, lam_init)
        o = acc_ref[:, :HEAD_DIM] / acc_ref[:, HEAD_DIM:]
        o_ref[...] = o[:tq] - lam * o[tq:]


def _attn_prompt(q, k, v, lam_p, lam_init, *, n_seq, seq_len, tq=512):
    m = q.shape[0]
    tq = _row_tile(seq_len, tq)
    nq = seq_len // tq
    pairs = [(i, j) for i in range(nq) for j in range(i + 1)]
    qi = jnp.asarray([p[0] for p in pairs], jnp.int32)
    kj = jnp.asarray([p[1] for p in pairs], jnp.int32)
    lam_pad = jnp.zeros((SUBLANES, LANES), F32).at[:4, :D_HEAD_B].set(lam_p)
    grid_spec = pltpu.PrefetchScalarGridSpec(
        num_scalar_prefetch=2,
        grid=(n_seq, N_HEADS, len(pairs)),
        in_specs=[pl.BlockSpec((SUBLANES, LANES), lambda b, h, p, qi, kj: (0, 0)),
                  pl.BlockSpec((tq, HEAD_DIM), lambda b, h, p, qi, kj: (b * nq + qi[p], h)),
                  pl.BlockSpec((tq, HEAD_DIM), lambda b, h, p, qi, kj: (b * nq + kj[p], h)),
                  pl.BlockSpec((tq, HEAD_DIM), lambda b, h, p, qi, kj: (b * nq + kj[p], h))],
        out_specs=pl.BlockSpec((tq, HEAD_DIM), lambda b, h, p, qi, kj: (b * nq + qi[p], h)),
        scratch_shapes=[pltpu.VMEM((2 * tq, HEAD_DIM), BF16), pltpu.VMEM((2 * tq, LANES), F32),
                        pltpu.VMEM((2 * tq, HEAD_DIM + LANES), F32)],
    )
    assert tq % LANES == 0
    return pl.pallas_call(
        functools.partial(_attn_prompt_kernel, lam_init=lam_init),
        grid_spec=grid_spec,
        out_shape=jax.ShapeDtypeStruct((m, D_MODEL), F32),
        compiler_params=_cparams(("parallel", "parallel", "arbitrary")),
        name="attn_prompt",
    )(qi, kj, lam_pad, q, k, v)


def _attn_sample_kernel(*refs, pages_per_step, t_real, lam_init):
    pp = pages_per_step
    pt_ref, lam_ref, q_ref, kn_ref, vn_ref = refs[:5]
    k_refs = refs[5:5 + pp]
    v_refs = refs[5 + pp:5 + 2 * pp]
    o_ref, qbd_ref, kpad_ref, vpad_ref, m_ref, l_ref, acc_ref = refs[5 + 2 * pp:]
    del pt_ref
    j = pl.program_id(1)
    n_rows = qbd_ref.shape[0]
    page = kpad_ref.shape[0]

    def update(s_list, v_list):
        m_prev = m_ref[...]
        m_cur = jnp.max(s_list[0], axis=1, keepdims=True)
        for s in s_list[1:]:
            m_cur = jnp.maximum(m_cur, jnp.max(s, axis=1, keepdims=True))
        m_new = jnp.maximum(m_prev, m_cur)
        alpha = jnp.exp2(m_prev - m_new)
        m_wide = jnp.concatenate([m_new] * (page // LANES), axis=1)
        l_add = None
        pv = None
        for s, vb in zip(s_list, v_list):
            pr = jnp.exp2(s - m_wide)
            row_sum = jnp.sum(pr, axis=1, keepdims=True)
            part = _dot(pr.astype(BF16), vb)
            l_add = row_sum if l_add is None else l_add + row_sum
            pv = part if pv is None else pv + part
        l_ref[...] = alpha * l_ref[...] + l_add
        acc_ref[...] = jnp.concatenate([alpha] * (D_MODEL // LANES), axis=1) * acc_ref[...] + pv
        m_ref[...] = m_new

    @pl.when(j == 0)
    def _init():
        rep = jnp.concatenate([q_ref[...]] * (n_rows // SAMPLE_PAD_T), axis=0)
        r_grp = lax.broadcasted_iota(jnp.int32, rep.shape, 0) // SAMPLE_PAD_T
        l_grp = lax.broadcasted_iota(jnp.int32, rep.shape, 1) // D_HEAD_B
        qbd_ref[...] = jnp.where(r_grp == l_grp, rep, 0.0).astype(BF16)
        m_ref[...] = jnp.full_like(m_ref, -jnp.inf)
        l_ref[...] = jnp.zeros_like(l_ref)
        acc_ref[...] = jnp.zeros_like(acc_ref)
        pad = jnp.zeros((page - SAMPLE_PAD_T, D_MODEL), F32)
        kpad_ref[...] = jnp.concatenate([kn_ref[...], pad], axis=0).astype(BF16)
        vpad_ref[...] = jnp.concatenate([vn_ref[...], pad], axis=0).astype(BF16)
        s = _dot_nt(qbd_ref[...], kpad_ref[...])
        t_q = lax.broadcasted_iota(jnp.int32, s.shape, 0) % SAMPLE_PAD_T
        t_k = lax.broadcasted_iota(jnp.int32, s.shape, 1)
        s = jnp.where((t_k <= t_q) & (t_k < t_real), s, -jnp.inf)
        update([s], [vpad_ref[...]])

    qbd = qbd_ref[...]
    update([_dot_nt(qbd, k_refs[p_i][0]) for p_i in range(pp)], [v_refs[p_i][0] for p_i in range(pp)])

    @pl.when(j == pl.num_programs(1) - 1)
    def _finish():
        lam = _lambda_value(lam_ref, lam_init)
        o = acc_ref[...] / jnp.concatenate([l_ref[...]] * (D_MODEL // LANES), axis=1)
        for h in range(N_HEADS):
            r1 = 2 * h * SAMPLE_PAD_T
            r2 = r1 + SAMPLE_PAD_T
            hs = slice(h * HEAD_DIM, (h + 1) * HEAD_DIM)
            o_ref[:, hs] = o[r1:r1 + SAMPLE_PAD_T, hs] - lam * o[r2:r2 + SAMPLE_PAD_T, hs]


def _attn_sample(q, k_new, v_new, cache_k, cache_v, page_table, lam_p, lam_init, *, t_real, pages_per_step=8):
    m = q.shape[0]
    n_seq, n_pages = page_table.shape
    page = cache_k.shape[1]
    pp = pages_per_step
    while n_pages % pp:
        pp //= 2
    lam_pad = jnp.zeros((SUBLANES, LANES), F32).at[:4, :D_HEAD_B].set(lam_p)
    n_rows = 2 * N_HEADS * SAMPLE_PAD_T
    seq = lambda s, j, pt: (s, 0)

    def page_spec(p_i):
        return pl.BlockSpec((1, page, D_MODEL), lambda s, j, pt: (pt[s * n_pages + j * pp + p_i], 0, 0))

    grid_spec = pltpu.PrefetchScalarGridSpec(
        num_scalar_prefetch=1,
        grid=(n_seq, n_pages // pp),
        in_specs=[pl.BlockSpec((SUBLANES, LANES), lambda s, j, pt: (0, 0)),
                  pl.BlockSpec((SAMPLE_PAD_T, D_MODEL), seq), pl.BlockSpec((SAMPLE_PAD_T, D_MODEL), seq),
                  pl.BlockSpec((SAMPLE_PAD_T, D_MODEL), seq)]
                 + [page_spec(p_i) for p_i in range(pp)] * 2,
        out_specs=pl.BlockSpec((SAMPLE_PAD_T, D_MODEL), seq),
        scratch_shapes=[pltpu.VMEM((n_rows, D_MODEL), BF16), pltpu.VMEM((page, D_MODEL), BF16),
                        pltpu.VMEM((page, D_MODEL), BF16), pltpu.VMEM((n_rows, LANES), F32),
                        pltpu.VMEM((n_rows, LANES), F32), pltpu.VMEM((n_rows, D_MODEL), F32)],
    )
    assert page % LANES == 0
    return pl.pallas_call(
        functools.partial(_attn_sample_kernel, pages_per_step=pp, t_real=t_real, lam_init=lam_init),
        grid_spec=grid_spec,
        out_shape=jax.ShapeDtypeStruct((m, D_MODEL), F32),
        compiler_params=_cparams(("parallel", "arbitrary")),
        name="attn_sample",
    )(page_table.reshape(-1), lam_pad, q, k_new, v_new, *([cache_k] * pp), *([cache_v] * pp))


def _rope_tables(pos):
    half = D_HEAD_B // 2
    inv_freq = ROPE_THETA ** (-jnp.arange(half, dtype=F32) / half)
    ang = pos.astype(F32)[:, None] * inv_freq[None, :]
    cos, sin = jnp.cos(ang), jnp.sin(ang)
    cos128 = jnp.tile(cos, (1, LANES // half))
    sin128 = jnp.tile(jnp.concatenate([-sin, sin], axis=1), (1, LANES // D_HEAD_B))
    return cos128, sin128


def _trunk(x, pos, conv_state, delta_state, past, wts, *, n_seq, seq_len, t_real):
    h = x
    hb = x.astype(BF16)
    cos128, sin128 = _rope_tables(pos)
    new_convs, new_states = [], []
    kf = vf = kb = vb = None
    for l in range(DEPTH):
        if l < DEPTH // 2:
            qkv_pre, z, ab = _linear(hb, wts["w_in"][l], (QKV_DIM, D_MODEL, LANES), (F32, BF16, F32))
            pre3 = qkv_pre.reshape(n_seq, seq_len, QKV_DIM)
            new_convs.append(pre3[:, t_real - (CONV_W - 1):t_real])
            if conv_state is None:
                prev = None
            else:
                prev = jnp.pad(conv_state[l], ((0, 0), (SAMPLE_PAD_T - (CONV_W - 1), 0), (0, 0)))
                prev = prev.reshape(n_seq * SAMPLE_PAD_T, QKV_DIM)
            q, k, v, gb = _conv_gate(qkv_pre, prev, ab, wts["w_conv"][l], wts["a_log"][l], wts["dt_bias"][l],
                                     seq_len=seq_len, t_real=t_real, chunk=DELTA_CHUNK)
            s0 = None if delta_state is None else delta_state[l]
            o, s_new = _delta_rule(q, k, v, gb, s0, n_seq=n_seq, seq_len=seq_len, chunk=DELTA_CHUNK)
            new_states.append(s_new)
            h, hb = _mix_out(o, z, h, wts["g_norm"][l], wts["w_out"][l], wts["ln_mix_g"][l], wts["ln_mix_b"][l])
        else:
            j = l - DEPTH // 2
            lam_init = 0.8 - 0.6 * math.exp(-0.3 * l)
            if past is None:
                q = _q_proj(hb, wts["w_q"][j], cos128, sin128, BF16)
                o = _attn_prompt(q, kb, vb, wts["lambda"][j], lam_init, n_seq=n_seq, seq_len=seq_len)
            else:
                q = _q_proj(hb, wts["w_q"][j], cos128, sin128, F32)
                o = _attn_sample(q, kf, vf, past[0], past[1], past[2], wts["lambda"][j], lam_init, t_real=t_real)
            h, hb = _mix_out(o, None, h, wts["g_sub"][j], wts["w_o"][j], wts["ln_mix_g"][l], wts["ln_mix_b"][l],
                             post_scale=1.0 - lam_init)
        if l % 2 == 0:
            i = l // 2
            h, hb = _dense_ffn(hb, h, wts["w_gate_d"][i], wts["w_up_d"][i], wts["w_down_d"][i],
                               wts["ln_ffn_g"][l], wts["ln_ffn_b"][l])
        else:
            i = l // 2
            h, hb = _moe_ffn(h, hb, wts["w_router"][i], wts["w_gate_e"][i], wts["w_up_e"][i], wts["w_down_e"][i],
                             wts["ln_ffn_g"][l], wts["ln_ffn_b"][l])
        if l == DEPTH // 2 - 1:
            kf, vf, kb, vb = _kv_proj(hb, wts["w_kv"], cos128, sin128)
    return h, jnp.stack(new_convs), jnp.stack(new_states), kf, vf


def kernel(x_prompt, x_sample, state_delta, state_conv, cache_k, cache_v, page_table, w_in_a, w_conv_a, a_log_a, dt_bias_a, g_norm_a, w_out_a, w_kv, w_q_b, lambda_b, g_sub_b, w_o_b, ln_mix_g, ln_mix_b, ln_ffn_g, ln_ffn_b, w_gate_d, w_up_d, w_down_d, w_router, w_gate_e, w_up_e, w_down_e):
    n_a = w_in_a.shape[0]
    bp, tp, _ = x_prompt.shape
    bs, ts, _ = x_sample.shape
    assert CONV_W - 1 <= ts <= SAMPLE_PAD_T and tp % SUBLANES == 0

    w_in = jnp.pad(w_in_a, ((0, 0), (0, 0), (0, LANES - 2 * N_HEADS))).astype(BF16)
    wts = dict(
        w_in=w_in, w_conv=w_conv_a, a_log=a_log_a, dt_bias=dt_bias_a, g_norm=g_norm_a,
        w_out=w_out_a.astype(BF16), w_kv=w_kv.astype(BF16), w_q=w_q_b.astype(BF16), g_sub=g_sub_b,
        w_o=w_o_b.astype(BF16), ln_mix_g=ln_mix_g, ln_mix_b=ln_mix_b, ln_ffn_g=ln_ffn_g, ln_ffn_b=ln_ffn_b,
        w_gate_d=w_gate_d.astype(BF16), w_up_d=w_up_d.astype(BF16), w_down_d=w_down_d.astype(BF16),
        w_router=w_router, w_gate_e=w_gate_e.astype(BF16), w_up_e=w_up_e.astype(BF16),
        w_down_e=w_down_e.astype(BF16))
    wts["lambda"] = lambda_b

    pos_p = jnp.tile(jnp.arange(tp), bp)
    y_p, conv_p, delta_p, k_p, v_p = _trunk(x_prompt.reshape(bp * tp, D_MODEL), pos_p, None, None, None, wts,
                                            n_seq=bp, seq_len=tp, t_real=tp)

    n_pages = page_table.shape[1]
    page = cache_k.shape[1]
    past_len = n_pages * page
    xs = jnp.pad(x_sample, ((0, 0), (0, SAMPLE_PAD_T - ts), (0, 0))).reshape(bs * SAMPLE_PAD_T, D_MODEL)
    pos_s = jnp.tile(past_len + jnp.arange(SAMPLE_PAD_T), bs)
    past = (cache_k.reshape(cache_k.shape[0], page, D_MODEL).astype(BF16),
            cache_v.reshape(cache_v.shape[0], page, D_MODEL).astype(BF16), page_table)
    y_s, conv_s, delta_s, k_s, v_s = _trunk(xs, pos_s, state_conv, state_delta, past, wts,
                                            n_seq=bs, seq_len=SAMPLE_PAD_T, t_real=ts)

    def unpad(a):
        return a.reshape(bs, SAMPLE_PAD_T, D_MODEL)[:, :ts]

    return (y_p.reshape(bp, tp, D_MODEL), unpad(y_s), delta_p, conv_p,
            k_p.reshape(bp, tp, N_HEADS, 2, D_HEAD_B), v_p.reshape(bp, tp, N_HEADS, HEAD_DIM),
            delta_s, conv_s,
            unpad(k_s).reshape(bs, ts, N_HEADS, 2, D_HEAD_B), unpad(v_s).reshape(bs, ts, N_HEADS, HEAD_DIM))
```

```python
import functools
import math

import jax
import jax.numpy as jnp
from jax import lax
from jax.experimental import pallas as pl
from jax.experimental.pallas import tpu as pltpu

F32 = jnp.float32
BF16 = jnp.bfloat16
HIGHEST = lax.Precision.HIGHEST

LANES = 128
SUBLANES = 8
BF16_ROWS = 16
VMEM_LIMIT = 56 * 1024 * 1024

D_MODEL = 1024
N_HEADS = 8
HEAD_DIM = 128
QKV_DIM = 3 * D_MODEL
CONV_W = 4
DELTA_CHUNK = 64
D_HEAD_B = 64
ROPE_THETA = 10000.0
N_EXPERTS = 8
DEPTH = 4
DEEPNORM_ALPHA = (2.0 * DEPTH) ** 0.25
LN_EPS = 1e-5
RMS_EPS = 1e-6
SAMPLE_PAD_T = 8


def _cparams(semantics):
    return pltpu.CompilerParams(dimension_semantics=semantics, vmem_limit_bytes=VMEM_LIMIT)


def _dot(a, b, precision=None):
    return jnp.dot(a, b, preferred_element_type=F32, precision=precision)


def _dot_nt(a, b, precision=None):
    return lax.dot_general(a, b, (((1,), (1,)), ((), ())), preferred_element_type=F32, precision=precision)


def _dot_tn(a, b, precision=None):
    return lax.dot_general(a, b, (((0,), (0,)), ((), ())), preferred_element_type=F32, precision=precision)


def _layer_norm(y, g, b):
    mu = jnp.mean(y, axis=-1, keepdims=True)
    d = y - mu
    var = jnp.mean(d * d, axis=-1, keepdims=True)
    return d * lax.rsqrt(var + LN_EPS) * g + b


def _silu(x):
    return x * jax.nn.sigmoid(x)


def _row_tile(m, pref):
    t = min(pref, m)
    while m % t:
        t //= 2
    return t


def _linear_kernel(x_ref, w_ref, *out_refs, widths, tn):
    x = x_ref[...]
    col = 0
    for o_ref, width in zip(out_refs, widths):
        for c in range(0, width, tn):
            cw = min(tn, width - c)
            o_ref[:, c:c + cw] = _dot(x, w_ref[:, col + c:col + c + cw]).astype(o_ref.dtype)
        col += width


def _linear(x, w, widths, dtypes, tm=256, tn=512):
    m, k = x.shape
    tm = _row_tile(m, tm)
    n = sum(widths)
    return pl.pallas_call(
        functools.partial(_linear_kernel, widths=tuple(widths), tn=tn),
        grid=(m // tm,),
        in_specs=[pl.BlockSpec((tm, k), lambda i: (i, 0)),
                  pl.BlockSpec((k, n), lambda i: (0, 0))],
        out_specs=[pl.BlockSpec((tm, wd), lambda i: (i, 0)) for wd in widths],
        out_shape=[jax.ShapeDtypeStruct((m, wd), dt) for wd, dt in zip(widths, dtypes)],
        compiler_params=_cparams(("parallel",)),
        name="linear",
    )(x, w)


def _rope(x, cos128, sin128):
    n = x.shape[1]
    reps = n // LANES
    cos_t = jnp.concatenate([cos128] * reps, axis=1)
    sin_t = jnp.concatenate([sin128] * reps, axis=1)
    lane = lax.broadcasted_iota(jnp.int32, x.shape, 1)
    first_half = (lane % D_HEAD_B) < (D_HEAD_B // 2)
    rot = jnp.where(first_half, pltpu.roll(x, n - D_HEAD_B // 2, 1), pltpu.roll(x, D_HEAD_B // 2, 1))
    return x * cos_t + rot * sin_t


def _q_proj_kernel(x_ref, w_ref, cos_ref, sin_ref, q_ref, *, scale):
    q = _dot(x_ref[...], w_ref[...])
    q_ref[...] = (_rope(q, cos_ref[...], sin_ref[...]) * scale).astype(q_ref.dtype)


def _q_proj(x, w, cos, sin, out_dtype, tm=256):
    m, k = x.shape
    tm = _row_tile(m, tm)
    return pl.pallas_call(
        functools.partial(_q_proj_kernel, scale=D_HEAD_B ** -0.5 * math.log2(math.e)),
        grid=(m // tm,),
        in_specs=[pl.BlockSpec((tm, k), lambda i: (i, 0)),
                  pl.BlockSpec((k, D_MODEL), lambda i: (0, 0)),
                  pl.BlockSpec((tm, LANES), lambda i: (i, 0)),
                  pl.BlockSpec((tm, LANES), lambda i: (i, 0))],
        out_specs=pl.BlockSpec((tm, D_MODEL), lambda i: (i, 0)),
        out_shape=jax.ShapeDtypeStruct((m, D_MODEL), out_dtype),
        compiler_params=_cparams(("parallel",)),
        name="q_proj",
    )(x, w, cos, sin)


def _kv_proj_kernel(x_ref, w_ref, cos_ref, sin_ref, k_ref, v_ref, kb_ref, vb_ref):
    x = x_ref[...]
    k = _rope(_dot(x, w_ref[:, :D_MODEL]), cos_ref[...], sin_ref[...])
    v = _dot(x, w_ref[:, D_MODEL:])
    k_ref[...] = k
    v_ref[...] = v
    kb_ref[...] = k.astype(BF16)
    vb_ref[...] = v.astype(BF16)


def _kv_proj(x, w, cos, sin, tm=256):
    m, k = x.shape
    tm = _row_tile(m, tm)
    row = lambda i: (i, 0)
    return pl.pallas_call(
        _kv_proj_kernel,
        grid=(m // tm,),
        in_specs=[pl.BlockSpec((tm, k), row),
                  pl.BlockSpec((k, 2 * D_MODEL), lambda i: (0, 0)),
                  pl.BlockSpec((tm, LANES), row),
                  pl.BlockSpec((tm, LANES), row)],
        out_specs=[pl.BlockSpec((tm, D_MODEL), row)] * 4,
        out_shape=[jax.ShapeDtypeStruct((m, D_MODEL), F32)] * 2 + [jax.ShapeDtypeStruct((m, D_MODEL), BF16)] * 2,
        compiler_params=_cparams(("parallel",)),
        name="kv_proj",
    )(x, w, cos, sin)


def _conv_gate_kernel(x_ref, p_ref, ab_ref, cw_ref, alog_ref, dtb_ref, q_ref, k_ref, v_ref, gb_ref,
                      *, seg, t_real, chunk, blocks_per_seq):
    tm = x_ref.shape[0]
    row = lax.broadcasted_iota(jnp.int32, (tm, 1), 0)
    tloc = row % seg if seg < tm else row
    valid = tloc < t_real
    if seg == tm:
        not_first = (pl.program_id(0) % blocks_per_seq) != 0
        row8 = lax.broadcasted_iota(jnp.int32, (SUBLANES, 1), 0)

    for c in range(QKV_DIM // LANES):
        cs = slice(c * LANES, (c + 1) * LANES)
        xc = x_ref[:, cs]
        acc = cw_ref[CONV_W - 1:CONV_W, cs] * xc
        for j in range(1, CONV_W):
            sh = pltpu.roll(xc, j, 0)
            if seg == tm:
                halo = jnp.where(not_first, p_ref[:, cs], 0.0)
                head = jnp.where(row8 < j, pltpu.roll(halo, j, 0), sh[:SUBLANES])
                sh = jnp.concatenate([head, sh[SUBLANES:]], axis=0)
            else:
                prev = pltpu.roll(p_ref[:, cs], tm + j - SUBLANES, 0)
                sh = jnp.where(tloc >= j, sh, prev)
            acc = acc + cw_ref[CONV_W - 1 - j:CONV_W - j, cs] * sh
        y = _silu(acc)
        if c < 2 * N_HEADS:
            y = y * lax.rsqrt(jnp.sum(y * y, axis=-1, keepdims=True) + 1e-6)
        if c < N_HEADS:
            y = y * (HEAD_DIM ** -0.5)
        y = jnp.where(valid, y, 0.0)
        if c < N_HEADS:
            q_ref[:, cs] = y
        elif c < 2 * N_HEADS:
            k_ref[:, (c - N_HEADS) * LANES:(c - N_HEADS + 1) * LANES] = y
        else:
            v_ref[:, (c - 2 * N_HEADS) * LANES:(c - 2 * N_HEADS + 1) * LANES] = y

    ab = ab_ref[...]
    z = ab + dtb_ref[...]
    softplus = jnp.maximum(z, 0.0) + jnp.log1p(jnp.exp(-jnp.abs(z)))
    g = jnp.where(valid, -jnp.exp(alog_ref[...]) * softplus, 0.0)
    beta = jnp.where(valid, jax.nn.sigmoid(ab), 0.0)
    ri = lax.broadcasted_iota(jnp.int32, (tm, tm), 0)
    ci = lax.broadcasted_iota(jnp.int32, (tm, tm), 1)
    tril = jnp.where((ri // chunk == ci // chunk) & (ci <= ri), 1.0, 0.0)
    g_cum = _dot(tril, g, precision=HIGHEST)
    lane = lax.broadcasted_iota(jnp.int32, (tm, LANES), 1)
    gb_ref[...] = jnp.where(lane < N_HEADS, g_cum, beta)


def _conv_gate(qkv_pre, prev, ab, conv_w, a_log, dt_bias, *, seq_len, t_real, chunk, tm=256):
    m = qkv_pre.shape[0]
    if prev is None:
        tm = _row_tile(seq_len, tm)
        seg = tm
        blocks_per_seq = seq_len // tm
        tiles_per_block = tm // SUBLANES
        p_arr = qkv_pre
        p_spec = pl.BlockSpec((SUBLANES, QKV_DIM), lambda i: (jnp.maximum(i * tiles_per_block - 1, 0), 0))
    else:
        tm = _row_tile(m, tm)
        seg = seq_len
        blocks_per_seq = 1
        p_arr = prev
        p_spec = pl.BlockSpec((tm, QKV_DIM), lambda i: (i, 0))
    assert seg % chunk == 0 or chunk % seg == 0
    row = lambda i: (i, 0)
    const = lambda i: (0, 0)
    cw = jnp.zeros((SUBLANES, QKV_DIM), F32).at[:CONV_W].set(conv_w)
    alog = jnp.zeros((1, LANES), F32).at[0, :N_HEADS].set(a_log)
    dtb = jnp.zeros((1, LANES), F32).at[0, :N_HEADS].set(dt_bias)
    return pl.pallas_call(
        functools.partial(_conv_gate_kernel, seg=seg, t_real=t_real, chunk=min(chunk, seg),
                          blocks_per_seq=blocks_per_seq),
        grid=(m // tm,),
        in_specs=[pl.BlockSpec((tm, QKV_DIM), row), p_spec, pl.BlockSpec((tm, LANES), row),
                  pl.BlockSpec((SUBLANES, QKV_DIM), const), pl.BlockSpec((1, LANES), const),
                  pl.BlockSpec((1, LANES), const)],
        out_specs=[pl.BlockSpec((tm, D_MODEL), row)] * 3 + [pl.BlockSpec((tm, LANES), row)],
        out_shape=[jax.ShapeDtypeStruct((m, D_MODEL), F32)] * 3 + [jax.ShapeDtypeStruct((m, LANES), F32)],
        compiler_params=_cparams(("parallel",)),
        name="conv_gate",
    )(qkv_pre, p_arr, ab, cw, alog, dtb)


def _delta_prepare(qs, ks, vs, gs, bs):
    n = len(qs)
    c = qs[0].shape[0]
    ri = lax.broadcasted_iota(jnp.int32, (c, c), 0)
    ci = lax.broadcasted_iota(jnp.int32, (c, c), 1)
    causal = ci <= ri
    strict = ci < ri
    eye = ci == ri
    decay, kb, k_b, pw = [], [], [], []
    for i in range(n):
        g_row = jnp.sum(jnp.where(eye, gs[i], 0.0), axis=0, keepdims=True)
        decay.append(jnp.where(causal, jnp.exp(jnp.where(causal, gs[i] - g_row, 0.0)), 0.0))
        kb.append(ks[i] * bs[i])
        k_b.append(ks[i].astype(BF16))
    for i in range(n):
        pw.append(-jnp.where(strict, _dot_nt(kb[i].astype(BF16), k_b[i]) * decay[i], 0.0))
    inv_m1 = list(pw)
    for _ in range(int(math.log2(c)) - 1):
        for i in range(n):
            pw_b = pw[i].astype(BF16)
            pw[i] = _dot(pw_b, pw_b)
        for i in range(n):
            inv_m1[i] = inv_m1[i] + pw[i] + _dot(inv_m1[i].astype(BF16), pw[i].astype(BF16))
    out = []
    for i in range(n):
        e_g = jnp.exp(gs[i])
        rhs = jnp.concatenate([vs[i] * bs[i], kb[i] * e_g], axis=1)
        sol = rhs + _dot(inv_m1[i].astype(BF16), rhs.astype(BF16))
        attn = _dot_nt(qs[i].astype(BF16), k_b[i]) * decay[i]
        g_last = gs[i][c - 1:c, :]
        out.append(dict(u=sol[:, :HEAD_DIM], w=sol[:, HEAD_DIM:].astype(BF16), attn=attn.astype(BF16),
                        q_dec=(qs[i] * e_g).astype(BF16),
                        k_dec=(ks[i] * jnp.exp(g_last - gs[i])).astype(BF16), decay_last=jnp.exp(g_last)))
    return out


def _delta_apply(prep, states):
    n = len(prep)
    sb = [s.astype(BF16) for s in states]
    v_new = [(prep[i]["u"] - _dot(prep[i]["w"], sb[i])).astype(BF16) for i in range(n)]
    outs = [_dot(prep[i]["q_dec"], sb[i]) + _dot(prep[i]["attn"], v_new[i]) for i in range(n)]
    new_states = [states[i] * prep[i]["decay_last"] + _dot_tn(prep[i]["k_dec"], v_new[i]) for i in range(n)]
    return outs, new_states


def _delta_kernel(*refs, chunk, has_s0):
    if has_s0:
        q_ref, k_ref, v_ref, gb_ref, s0_ref, o_ref, s_ref = refs
    else:
        q_ref, k_ref, v_ref, gb_ref, o_ref, s_ref = refs
        s0_ref = None
    rows = q_ref.shape[0]

    @pl.when(pl.program_id(1) == 0)
    def _init():
        if has_s0:
            s_ref[...] = s0_ref[...]
        else:
            s_ref[...] = jnp.zeros_like(s_ref)

    n_chunks = rows // chunk
    qs, ks, vs, gs, bs = [], [], [], [], []
    for ci in range(n_chunks):
        rs = slice(ci * chunk, (ci + 1) * chunk)
        gb = gb_ref[rs, :]
        for h in range(N_HEADS):
            hs = slice(h * HEAD_DIM, (h + 1) * HEAD_DIM)
            qs.append(q_ref[rs, hs])
            ks.append(k_ref[rs, hs])
            vs.append(v_ref[rs, hs])
            gs.append(gb[:, h:h + 1])
            bs.append(gb[:, N_HEADS + h:N_HEADS + h + 1])
    prep = _delta_prepare(qs, ks, vs, gs, bs)
    states = [s_ref[0, h] for h in range(N_HEADS)]
    for ci in range(n_chunks):
        outs, states = _delta_apply(prep[ci * N_HEADS:(ci + 1) * N_HEADS], states)
        for h in range(N_HEADS):
            o_ref[ci * chunk:(ci + 1) * chunk, h * HEAD_DIM:(h + 1) * HEAD_DIM] = outs[h]
    for h in range(N_HEADS):
        s_ref[0, h] = states[h]


def _delta_rule(q, k, v, gb, s0, *, n_seq, seq_len, chunk, rows_per_step=128):
    m = q.shape[0]
    chunk = min(chunk, seq_len)
    rows = _row_tile(seq_len, max(rows_per_step, chunk))
    steps = seq_len // rows
    row = lambda s, i: (s * steps + i, 0)
    st = lambda s, i: (s, 0, 0, 0)
    in_specs = [pl.BlockSpec((rows, D_MODEL), row)] * 3 + [pl.BlockSpec((rows, LANES), row)]
    args = [q, k, v, gb]
    if s0 is not None:
        in_specs.append(pl.BlockSpec((1, N_HEADS, HEAD_DIM, HEAD_DIM), st))
        args.append(s0)
    return pl.pallas_call(
        functools.partial(_delta_kernel, chunk=chunk, has_s0=s0 is not None),
        grid=(n_seq, steps),
        in_specs=in_specs,
        out_specs=[pl.BlockSpec((rows, D_MODEL), row), pl.BlockSpec((1, N_HEADS, HEAD_DIM, HEAD_DIM), st)],
        out_shape=[jax.ShapeDtypeStruct((m, D_MODEL), F32),
                   jax.ShapeDtypeStruct((n_seq, N_HEADS, HEAD_DIM, HEAD_DIM), F32)],
        compiler_params=_cparams(("parallel", "arbitrary")),
        name="delta_rule",
    )(*args)


def _mix_out_kernel(*refs, gated, post_scale):
    if gated:
        o_ref, z_ref, h_ref, gn_ref, w_ref, lg_ref, lb_ref, hf_ref, hb_ref = refs
    else:
        o_ref, h_ref, gn_ref, w_ref, lg_ref, lb_ref, hf_ref, hb_ref = refs
    parts = []
    for hd in range(N_HEADS):
        hs = slice(hd * HEAD_DIM, (hd + 1) * HEAD_DIM)
        o = o_ref[:, hs]
        y = o * lax.rsqrt(jnp.mean(o * o, axis=-1, keepdims=True) + RMS_EPS) * gn_ref[:, hs]
        if gated:
            y = y * _silu(z_ref[:, hs].astype(F32))
        else:
            y = y * post_scale
        parts.append(y.astype(BF16))
    mix = _dot(jnp.concatenate(parts, axis=1), w_ref[...])
    hn = _layer_norm(DEEPNORM_ALPHA * h_ref[...] + mix, lg_ref[...], lb_ref[...])
    hf_ref[...] = hn
    hb_ref[...] = hn.astype(BF16)


def _mix_out(o, z, h, g_norm, w, ln_g, ln_b, *, post_scale=1.0, tm=256):
    m = o.shape[0]
    tm = _row_tile(m, tm)
    row = lambda i: (i, 0)
    const = lambda i: (0, 0)
    gated = z is not None
    gn = jnp.tile(g_norm.reshape(1, HEAD_DIM), (1, N_HEADS))
    args = [o] + ([z] if gated else []) + [h, gn, w, ln_g.reshape(1, -1), ln_b.reshape(1, -1)]
    in_specs = ([pl.BlockSpec((tm, D_MODEL), row)] * (3 if gated else 2)
                + [pl.BlockSpec((1, D_MODEL), const), pl.BlockSpec((D_MODEL, D_MODEL), const),
                   pl.BlockSpec((1, D_MODEL), const), pl.BlockSpec((1, D_MODEL), const)])
    return pl.pallas_call(
        functools.partial(_mix_out_kernel, gated=gated, post_scale=post_scale),
        grid=(m // tm,),
        in_specs=in_specs,
        out_specs=[pl.BlockSpec((tm, D_MODEL), row)] * 2,
        out_shape=[jax.ShapeDtypeStruct((m, D_MODEL), F32), jax.ShapeDtypeStruct((m, D_MODEL), BF16)],
        compiler_params=_cparams(("parallel",)),
        name="mix_out",
    )(*args)


def _dense_ffn_kernel(xb_ref, h_ref, wg_ref, wu_ref, wd_ref, lg_ref, lb_ref, hf_ref, hb_ref, *, tf):
    x = xb_ref[...]
    f_dim = wg_ref.shape[1]
    acc = DEEPNORM_ALPHA * h_ref[...]
    for c in range(0, f_dim, tf):
        cw = min(tf, f_dim - c)
        g = _dot(x, wg_ref[:, c:c + cw])
        u = _dot(x, wu_ref[:, c:c + cw])
        acc = acc + _dot((_silu(g) * u).astype(BF16), wd_ref[c:c + cw, :])
    hn = _layer_norm(acc, lg_ref[...], lb_ref[...])
    hf_ref[...] = hn
    hb_ref[...] = hn.astype(BF16)


def _dense_ffn(xb, h, wg, wu, wd, ln_g, ln_b, *, tm=256, tf=256):
    m = xb.shape[0]
    f_dim = wg.shape[1]
    tm = _row_tile(m, tm)
    row = lambda i: (i, 0)
    const = lambda i: (0, 0)
    return pl.pallas_call(
        functools.partial(_dense_ffn_kernel, tf=tf),
        grid=(m // tm,),
        in_specs=[pl.BlockSpec((tm, D_MODEL), row), pl.BlockSpec((tm, D_MODEL), row),
                  pl.BlockSpec((D_MODEL, f_dim), const), pl.BlockSpec((D_MODEL, f_dim), const),
                  pl.BlockSpec((f_dim, D_MODEL), const),
                  pl.BlockSpec((1, D_MODEL), const), pl.BlockSpec((1, D_MODEL), const)],
        out_specs=[pl.BlockSpec((tm, D_MODEL), row)] * 2,
        out_shape=[jax.ShapeDtypeStruct((m, D_MODEL), F32), jax.ShapeDtypeStruct((m, D_MODEL), BF16)],
        compiler_params=_cparams(("parallel",)),
        name="dense_ffn",
    )(xb, h, wg, wu, wd, ln_g.reshape(1, -1), ln_b.reshape(1, -1))


MOE_TM = 256
MOE_BM = 512
MOE_ALIGN = BF16_ROWS


def _router_kernel(h_ref, wr_ref, a_ref, r_ref, g_ref, cnt_ref):
    tm = h_ref.shape[0]
    logits = _dot_nt(wr_ref[...], h_ref[...], precision=HIGHEST)
    e_idx = lax.broadcasted_iota(jnp.int32, logits.shape, 0).astype(F32)
    m1 = jnp.max(logits, axis=0, keepdims=True)
    i1 = jnp.min(jnp.where(logits == m1, e_idx, float(N_EXPERTS)), axis=0, keepdims=True)
    first = e_idx == i1
    rest = jnp.where(first, -jnp.inf, logits)
    m2 = jnp.max(rest, axis=0, keepdims=True)
    i2 = jnp.min(jnp.where(rest == m2, e_idx, float(N_EXPERTS)), axis=0, keepdims=True)
    second = e_idx == i2
    ex = jnp.exp(m2 - m1)
    g1 = 1.0 / (1.0 + ex)
    g2 = ex / (1.0 + ex)
    routed = jnp.where(first | second, 1.0, 0.0)
    ji = lax.broadcasted_iota(jnp.int32, (tm, tm), 0)
    ii = lax.broadcasted_iota(jnp.int32, (tm, tm), 1)
    before = jnp.where(ji < ii, 1.0, 0.0).astype(BF16)
    a_ref[...] = routed
    r_ref[...] = _dot(routed.astype(BF16), before)
    g_ref[...] = jnp.where(first, g1, 0.0) + jnp.where(second, g2, 0.0)
    cnt_ref[0] = jnp.broadcast_to(jnp.sum(routed, axis=1, keepdims=True), (N_EXPERTS, LANES))


def _router(h, w_router_t, tm):
    m = h.shape[0]
    nb = m // tm
    col = lambda i: (0, i)
    return pl.pallas_call(
        _router_kernel,
        grid=(nb,),
        in_specs=[pl.BlockSpec((tm, D_MODEL), lambda i: (i, 0)),
                  pl.BlockSpec((N_EXPERTS, D_MODEL), lambda i: (0, 0))],
        out_specs=[pl.BlockSpec((N_EXPERTS, tm), col)] * 3
                  + [pl.BlockSpec((1, N_EXPERTS, LANES), lambda i: (i, 0, 0))],
        out_shape=[jax.ShapeDtypeStruct((N_EXPERTS, m), F32)] * 3
                  + [jax.ShapeDtypeStruct((nb, N_EXPERTS, LANES), F32)],
        compiler_params=_cparams(("parallel",)),
        name="moe_router",
    )(h, w_router_t)


def _dispatch_kernel(off_ref, xb_ref, a_ref, r_ref, rows_in_ref, rows_ref, slab_ref, sem_ref):
    del rows_in_ref
    b = pl.program_id(0)
    nb = pl.num_programs(0)
    slot = b % 2
    tm = xb_ref.shape[0]

    def slab_copy(which_slot, which_block, e):
        off = pl.multiple_of(off_ref[which_block * N_EXPERTS + e], MOE_ALIGN)
        return pltpu.make_async_copy(slab_ref.at[which_slot, e], rows_ref.at[pl.ds(off, tm)],
                                     sem_ref.at[which_slot, e])

    x = xb_ref[...]
    r_idx = lax.broadcasted_iota(jnp.int32, (tm, tm), 0).astype(F32)
    for e in range(N_EXPERTS):
        pick = jnp.where((r_idx == r_ref[e:e + 1, :]) & (a_ref[e:e + 1, :] > 0.0), 1.0, 0.0)
        slab_ref[slot, e] = _dot(pick.astype(BF16), x).astype(BF16)

    @pl.when(b > 0)
    def _previous_block_landed():
        for e in range(N_EXPERTS):
            slab_copy(1 - slot, b - 1, e).wait()

    for e in range(N_EXPERTS):
        slab_copy(slot, b, e).start()

    @pl.when(b == nb - 1)
    def _drain():
        for e in range(N_EXPERTS):
            slab_copy(slot, b, e).wait()


def _dispatch(xb, a_t, r_t, off, n_rows, tm):
    m = xb.shape[0]
    nb = m // tm
    rows_init = jnp.zeros((n_rows, D_MODEL), BF16)
    grid_spec = pltpu.PrefetchScalarGridSpec(
        num_scalar_prefetch=1,
        grid=(nb,),
        in_specs=[pl.BlockSpec((tm, D_MODEL), lambda b, off: (b, 0)),
                  pl.BlockSpec((N_EXPERTS, tm), lambda b, off: (0, b)),
                  pl.BlockSpec((N_EXPERTS, tm), lambda b, off: (0, b)),
                  pl.BlockSpec(memory_space=pl.ANY)],
        out_specs=pl.BlockSpec(memory_space=pl.ANY),
        scratch_shapes=[pltpu.VMEM((2, N_EXPERTS, tm, D_MODEL), BF16),
                        pltpu.SemaphoreType.DMA((2, N_EXPERTS))],
    )
    return pl.pallas_call(
        _dispatch_kernel,
        grid_spec=grid_spec,
        out_shape=jax.ShapeDtypeStruct((n_rows, D_MODEL), BF16),
        input_output_aliases={4: 0},
        compiler_params=_cparams(("arbitrary",)),
        name="moe_dispatch",
    )(off, xb, a_t, r_t, rows_init)


def _expert_ffn_kernel(be_ref, valid_ref, x_ref, wg_ref, wu_ref, wd_ref, y_ref, acc_ref):
    del be_ref
    g_idx = pl.program_id(0)
    f = pl.program_id(1)
    nf = pl.num_programs(1)
    is_valid = valid_ref[g_idx] > 0

    @pl.when(is_valid)
    def _compute():
        x = x_ref[...]
        g = _dot(x, wg_ref[0])
        u = _dot(x, wu_ref[0])
        part = _dot((_silu(g) * u).astype(BF16), wd_ref[0])

        @pl.when(f == 0)
        def _():
            acc_ref[...] = part

        @pl.when(f > 0)
        def _():
            acc_ref[...] += part

        @pl.when(f == nf - 1)
        def _():
            y_ref[...] = acc_ref[...].astype(y_ref.dtype)

    @pl.when(jnp.logical_not(is_valid) & (f == nf - 1))
    def _skip():
        y_ref[...] = jnp.zeros_like(y_ref)


def _expert_ffn(x_rows, blk_e, valid, wg, wu, wd, bm, tf):
    n_rows = x_rows.shape[0]
    f_dim = wg.shape[2]
    nf = f_dim // tf
    grid_spec = pltpu.PrefetchScalarGridSpec(
        num_scalar_prefetch=2,
        grid=(n_rows // bm, nf),
        in_specs=[pl.BlockSpec((bm, D_MODEL), lambda g, f, be, va: (g, 0)),
                  pl.BlockSpec((1, D_MODEL, tf), lambda g, f, be, va: (be[g], 0, f)),
                  pl.BlockSpec((1, D_MODEL, tf), lambda g, f, be, va: (be[g], 0, f)),
                  pl.BlockSpec((1, tf, D_MODEL), lambda g, f, be, va: (be[g], f, 0))],
        out_specs=pl.BlockSpec((bm, D_MODEL), lambda g, f, be, va: (g, 0)),
        scratch_shapes=[pltpu.VMEM((bm, D_MODEL), F32)],
    )
    return pl.pallas_call(
        _expert_ffn_kernel,
        grid_spec=grid_spec,
        out_shape=jax.ShapeDtypeStruct((n_rows, D_MODEL), BF16),
        compiler_params=_cparams(("parallel", "arbitrary")),
        name="moe_expert_ffn",
    )(blk_e, valid, x_rows, wg, wu, wd)


def _combine_kernel(off_ref, h_ref, a_ref, r_ref, g_ref, lg_ref, lb_ref, rows_ref, hf_ref, hb_ref,
                    slab_ref, sem_ref, acc_ref):
    b = pl.program_id(0)
    nb = pl.num_programs(0)
    slot = b % 2
    tm = h_ref.shape[0]

    def slab_copy(which_slot, which_block, e):
        off = pl.multiple_of(off_ref[which_block * N_EXPERTS + e], MOE_ALIGN)
        return pltpu.make_async_copy(rows_ref.at[pl.ds(off, tm)], slab_ref.at[which_slot, e],
                                     sem_ref.at[which_slot, e])

    @pl.when(b == 0)
    def _prime():
        for e in range(N_EXPERTS):
            slab_copy(0, 0, e).start()

    @pl.when(b + 1 < nb)
    def _prefetch():
        for e in range(N_EXPERTS):
            slab_copy(1 - slot, b + 1, e).start()

    acc_ref[...] = DEEPNORM_ALPHA * h_ref[...]
    r_idx = lax.broadcasted_iota(jnp.int32, (tm, tm), 1).astype(F32)
    for e in range(N_EXPERTS):
        pick = jnp.where((r_idx == r_ref[:, e:e + 1]) & (a_ref[:, e:e + 1] > 0.0), 1.0, 0.0)
        slab_copy(slot, b, e).wait()
        acc_ref[...] += g_ref[:, e:e + 1] * _dot(pick.astype(BF16), slab_ref[slot, e])
    hn = _layer_norm(acc_ref[...], lg_ref[...], lb_ref[...])
    hf_ref[...] = hn
    hb_ref[...] = hn.astype(BF16)


def _combine(h, a_c, r_c, g_c, ln_g, ln_b, y_rows, off, tm):
    m = h.shape[0]
    nb = m // tm
    blk = lambda b, off: (b, 0)
    const = lambda b, off: (0, 0)
    grid_spec = pltpu.PrefetchScalarGridSpec(
        num_scalar_prefetch=1,
        grid=(nb,),
        in_specs=[pl.BlockSpec((tm, D_MODEL), blk),
                  pl.BlockSpec((tm, N_EXPERTS), blk), pl.BlockSpec((tm, N_EXPERTS), blk),
                  pl.BlockSpec((tm, N_EXPERTS), blk),
                  pl.BlockSpec((1, D_MODEL), const), pl.BlockSpec((1, D_MODEL), const),
                  pl.BlockSpec(memory_space=pl.ANY)],
        out_specs=[pl.BlockSpec((tm, D_MODEL), blk)] * 2,
        scratch_shapes=[pltpu.VMEM((2, N_EXPERTS, tm, D_MODEL), BF16), pltpu.SemaphoreType.DMA((2, N_EXPERTS)),
                        pltpu.VMEM((tm, D_MODEL), F32)],
    )
    return pl.pallas_call(
        _combine_kernel,
        grid_spec=grid_spec,
        out_shape=[jax.ShapeDtypeStruct((m, D_MODEL), F32), jax.ShapeDtypeStruct((m, D_MODEL), BF16)],
        compiler_params=_cparams(("arbitrary",)),
        name="moe_combine",
    )(off, h, a_c, r_c, g_c, ln_g.reshape(1, -1), ln_b.reshape(1, -1), y_rows)


def _moe_ffn(h, hb, w_router, wg, wu, wd, ln_g, ln_b):
    m = h.shape[0]
    tm = _row_tile(m, MOE_TM)
    nb = m // tm
    bm = MOE_BM
    f_dim = wg.shape[2]
    tf = f_dim // 2 if (f_dim // 2) % 256 == 0 else f_dim
    a_t, r_t, g_t, cnts = _router(h, w_router.T, tm)

    cnt = cnts[:, :, 0].astype(jnp.int32)
    padded = (cnt + MOE_ALIGN - 1) // MOE_ALIGN * MOE_ALIGN
    base = jnp.cumsum(padded, axis=0) - padded
    used = jnp.sum(padded, axis=0)
    cap = (used + tm + bm - 1) // bm * bm
    region = jnp.cumsum(cap) - cap
    off = (region[None, :] + base).reshape(-1).astype(jnp.int32)
    max_rows = 2 * m + nb * N_EXPERTS * (MOE_ALIGN - 1) + N_EXPERTS * (tm + bm)
    n_rows = (max_rows + bm - 1) // bm * bm
    g_idx = jnp.arange(n_rows // bm, dtype=jnp.int32)
    region_blk = region // bm
    blk_e = (jnp.sum(g_idx[:, None] >= region_blk[None, :], axis=1) - 1).astype(jnp.int32)
    n_blk = (used + bm - 1) // bm
    valid = ((g_idx - region_blk[blk_e]) < n_blk[blk_e]).astype(jnp.int32)

    x_rows = _dispatch(hb, a_t, r_t, off, n_rows, tm)
    y_rows = _expert_ffn(x_rows, blk_e, valid, wg, wu, wd, bm, tf)
    return _combine(h, a_t.T, r_t.T, g_t.T, ln_g, ln_b, y_rows, off, tm)


def _lambda_value(lam_ref, lam_init):
    s1 = jnp.sum(lam_ref[0:1, :] * lam_ref[1:2, :], axis=1, keepdims=True)
    s2 = jnp.sum(lam_ref[2:3, :] * lam_ref[3:4, :], axis=1, keepdims=True)
    return jnp.exp(s1) - jnp.exp(s2) + lam_init


def _attn_prompt_kernel(qi_ref, kj_ref, lam_ref, q_ref, k_ref, v_ref, o_ref, qq_ref, m_ref, acc_ref,
                        *, lam_init):
    p = pl.program_id(2)
    i = qi_ref[p]
    j = kj_ref[p]
    tq = q_ref.shape[0]
    tk = k_ref.shape[0]

    @pl.when(j == 0)
    def _init():
        q = q_ref[...]
        lane = lax.broadcasted_iota(jnp.int32, q.shape, 1)
        zero = jnp.zeros_like(q)
        qq_ref[0:tq, :] = jnp.where(lane < D_HEAD_B, q, zero)
        qq_ref[tq:2 * tq, :] = jnp.where(lane >= D_HEAD_B, q, zero)
        m_ref[...] = jnp.full_like(m_ref, -jnp.inf)
        acc_ref[...] = jnp.zeros_like(acc_ref)

    def step(masked):
        s = _dot_nt(qq_ref[...], k_ref[...])
        if masked:
            qpos = lax.broadcasted_iota(jnp.int32, s.shape, 0) % tq
            kpos = lax.broadcasted_iota(jnp.int32, s.shape, 1)
            s = jnp.where(kpos <= qpos, s, -jnp.inf)
        m_prev = m_ref[...]
        m_new = jnp.maximum(m_prev, jnp.max(s, axis=1, keepdims=True))
        alpha = jnp.exp2(m_prev - m_new)
        pr = jnp.exp2(s - jnp.concatenate([m_new] * (tk // LANES), axis=1))
        acc_ref[:, HEAD_DIM:] = alpha * acc_ref[:, HEAD_DIM:] + jnp.sum(pr, axis=1, keepdims=True)
        acc_ref[:, :HEAD_DIM] = alpha * acc_ref[:, :HEAD_DIM] + _dot(pr.astype(BF16), v_ref[...])
        m_ref[...] = m_new

    @pl.when(j < i)
    def _full():
        step(False)

    @pl.when(j == i)
    def _diag():
        step(True)
        lam = _lambda_value(lam_ref, lam_init)
        o = acc_ref[:, :HEAD_DIM] / acc_ref[:, HEAD_DIM:]
        o_ref[...] = o[:tq] - lam * o[tq:]


def _attn_prompt(q, k, v, lam_p, lam_init, *, n_seq, seq_len, tq=512):
    m = q.shape[0]
    tq = _row_tile(seq_len, tq)
    nq = seq_len // tq
    pairs = [(i, j) for i in range(nq) for j in range(i + 1)]
    qi = jnp.asarray([p[0] for p in pairs], jnp.int32)
    kj = jnp.asarray([p[1] for p in pairs], jnp.int32)
    lam_pad = jnp.zeros((SUBLANES, LANES), F32).at[:4, :D_HEAD_B].set(lam_p)
    grid_spec = pltpu.PrefetchScalarGridSpec(
        num_scalar_prefetch=2,
        grid=(n_seq, N_HEADS, len(pairs)),
        in_specs=[pl.BlockSpec((SUBLANES, LANES), lambda b, h, p, qi, kj: (0, 0)),
                  pl.BlockSpec((tq, HEAD_DIM), lambda b, h, p, qi, kj: (b * nq + qi[p], h)),
                  pl.BlockSpec((tq, HEAD_DIM), lambda b, h, p, qi, kj: (b * nq + kj[p], h)),
                  pl.BlockSpec((tq, HEAD_DIM), lambda b, h, p, qi, kj: (b * nq + kj[p], h))],
        out_specs=pl.BlockSpec((tq, HEAD_DIM), lambda b, h, p, qi, kj: (b * nq + qi[p], h)),
        scratch_shapes=[pltpu.VMEM((2 * tq, HEAD_DIM), BF16), pltpu.VMEM((2 * tq, LANES), F32),
                        pltpu.VMEM((2 * tq, HEAD_DIM + LANES), F32)],
    )
    assert tq % LANES == 0
    return pl.pallas_call(
        functools.partial(_attn_prompt_kernel, lam_init=lam_init),
        grid_spec=grid_spec,
        out_shape=jax.ShapeDtypeStruct((m, D_MODEL), F32),
        compiler_params=_cparams(("parallel", "parallel", "arbitrary")),
        name="attn_prompt",
    )(qi, kj, lam_pad, q, k, v)


def _attn_sample_kernel(*refs, pages_per_step, t_real, lam_init):
    pp = pages_per_step
    pt_ref, lam_ref, q_ref, kn_ref, vn_ref = refs[:5]
    k_refs = refs[5:5 + pp]
    v_refs = refs[5 + pp:5 + 2 * pp]
    o_ref, qbd_ref, spread_ref, m_ref, l_ref, acc_ref = refs[5 + 2 * pp:]
    del pt_ref
    j = pl.program_id(1)
    n_rows = qbd_ref.shape[0]
    page = k_refs[0].shape[1]
    rows_per_head = 2 * SAMPLE_PAD_T

    def online_update(s_list, pv_fn):
        m_prev = m_ref[...]
        m_cur = jnp.max(s_list[0], axis=1, keepdims=True)
        for s in s_list[1:]:
            m_cur = jnp.maximum(m_cur, jnp.max(s, axis=1, keepdims=True))
        m_new = jnp.maximum(m_prev, m_cur)
        alpha = jnp.exp2(m_prev - m_new)
        m_wide = jnp.concatenate([m_new] * (page // LANES), axis=1)
        prs = [jnp.exp2(s - m_wide) for s in s_list]
        l_add = jnp.sum(prs[0], axis=1, keepdims=True)
        for pr in prs[1:]:
            l_add = l_add + jnp.sum(pr, axis=1, keepdims=True)
        l_ref[...] = alpha * l_ref[...] + l_add
        acc_ref[...] = alpha * acc_ref[...] + pv_fn([pr.astype(BF16) for pr in prs])
        m_ref[...] = m_new

    @pl.when(j == 0)
    def _init():
        rep = jnp.concatenate([q_ref[...]] * (n_rows // SAMPLE_PAD_T), axis=0)
        r_grp = lax.broadcasted_iota(jnp.int32, rep.shape, 0) // SAMPLE_PAD_T
        l_grp = lax.broadcasted_iota(jnp.int32, rep.shape, 1) // D_HEAD_B
        qbd_ref[...] = jnp.where(r_grp == l_grp, rep, 0.0).astype(BF16)
        key = lax.broadcasted_iota(jnp.int32, spread_ref.shape, 0)
        col = lax.broadcasted_iota(jnp.int32, spread_ref.shape, 1)
        spread_ref[...] = jnp.where(col // N_HEADS == key, 1.0, 0.0).astype(BF16)
        m_ref[...] = jnp.full_like(m_ref, -jnp.inf)
        l_ref[...] = jnp.zeros_like(l_ref)
        acc_ref[...] = jnp.zeros_like(acc_ref)
        pad = jnp.zeros((page - SAMPLE_PAD_T, D_MODEL), F32)
        k_pad = jnp.concatenate([kn_ref[...], pad], axis=0).astype(BF16)
        v_pad = jnp.concatenate([vn_ref[...], pad], axis=0).astype(BF16)
        s = _dot_nt(qbd_ref[...], k_pad)
        t_q = lax.broadcasted_iota(jnp.int32, s.shape, 0) % SAMPLE_PAD_T
        t_k = lax.broadcasted_iota(jnp.int32, s.shape, 1)
        s = jnp.where((t_k <= t_q) & (t_k < t_real), s, -jnp.inf)

        def pv_new(prs):
            return jnp.concatenate(
                [_dot(prs[0][h * rows_per_head:(h + 1) * rows_per_head],
                      v_pad[:, h * HEAD_DIM:(h + 1) * HEAD_DIM]) for h in range(N_HEADS)], axis=0)

        online_update([s], pv_new)

    def pv_pages(prs):
        row_head = lax.broadcasted_iota(jnp.int32, (n_rows, page * N_HEADS), 0) // rows_per_head
        col_head = lax.broadcasted_iota(jnp.int32, (n_rows, page * N_HEADS), 1) % N_HEADS
        own_head = row_head == col_head
        total = None
        for p_i in range(pp):
            wide = jnp.where(own_head, _dot(prs[p_i], spread_ref[...]), 0.0).astype(BF16)
            part = _dot(wide, v_refs[p_i][0].astype(BF16))
            total = part if total is None else total + part
        return total

    qbd = qbd_ref[...]
    online_update([_dot_nt(qbd, k_refs[p_i][0].astype(BF16)) for p_i in range(pp)], pv_pages)

    @pl.when(j == pl.num_programs(1) - 1)
    def _finish():
        lam = _lambda_value(lam_ref, lam_init)
        o = acc_ref[...] / l_ref[...]
        for h in range(N_HEADS):
            r1 = h * rows_per_head
            r2 = r1 + SAMPLE_PAD_T
            o_ref[:, h * HEAD_DIM:(h + 1) * HEAD_DIM] = o[r1:r1 + SAMPLE_PAD_T] - lam * o[r2:r2 + SAMPLE_PAD_T]


def _attn_sample(q, k_new, v_new, cache_k, cache_v, page_table, lam_p, lam_init, *, t_real, pages_per_step=8):
    m = q.shape[0]
    n_seq, n_pages = page_table.shape
    page = cache_k.shape[1]
    pp = pages_per_step
    while n_pages % pp:
        pp //= 2
    lam_pad = jnp.zeros((SUBLANES, LANES), F32).at[:4, :D_HEAD_B].set(lam_p)
    n_rows = 2 * N_HEADS * SAMPLE_PAD_T
    seq = lambda s, j, pt: (s, 0)

    def page_spec(p_i, rows, width):
        return pl.BlockSpec((1, rows, width), lambda s, j, pt: (pt[s * n_pages + j * pp + p_i], 0, 0))

    grid_spec = pltpu.PrefetchScalarGridSpec(
        num_scalar_prefetch=1,
        grid=(n_seq, n_pages // pp),
        in_specs=[pl.BlockSpec((SUBLANES, LANES), lambda s, j, pt: (0, 0)),
                  pl.BlockSpec((SAMPLE_PAD_T, D_MODEL), seq), pl.BlockSpec((SAMPLE_PAD_T, D_MODEL), seq),
                  pl.BlockSpec((SAMPLE_PAD_T, D_MODEL), seq)]
                 + [page_spec(p_i, page, D_MODEL) for p_i in range(pp)]
                 + [page_spec(p_i, page * N_HEADS, HEAD_DIM) for p_i in range(pp)],
        out_specs=pl.BlockSpec((SAMPLE_PAD_T, D_MODEL), seq),
        scratch_shapes=[pltpu.VMEM((n_rows, D_MODEL), BF16), pltpu.VMEM((page, page * N_HEADS), BF16),
                        pltpu.VMEM((n_rows, LANES), F32), pltpu.VMEM((n_rows, LANES), F32),
                        pltpu.VMEM((n_rows, HEAD_DIM), F32)],
    )
    assert page == LANES and n_rows == LANES
    return pl.pallas_call(
        functools.partial(_attn_sample_kernel, pages_per_step=pp, t_real=t_real, lam_init=lam_init),
        grid_spec=grid_spec,
        out_shape=jax.ShapeDtypeStruct((m, D_MODEL), F32),
        compiler_params=_cparams(("parallel", "arbitrary")),
        name="attn_sample",
    )(page_table.reshape(-1), lam_pad, q, k_new, v_new, *([cache_k] * pp), *([cache_v] * pp))


def _rope_tables(pos):
    half = D_HEAD_B // 2
    inv_freq = ROPE_THETA ** (-jnp.arange(half, dtype=F32) / half)
    ang = pos.astype(F32)[:, None] * inv_freq[None, :]
    cos, sin = jnp.cos(ang), jnp.sin(ang)
    cos128 = jnp.tile(cos, (1, LANES // half))
    sin128 = jnp.tile(jnp.concatenate([-sin, sin], axis=1), (1, LANES // D_HEAD_B))
    return cos128, sin128


def _trunk(x, pos, conv_state, delta_state, past, wts, *, n_seq, seq_len, t_real):
    h = x
    hb = x.astype(BF16)
    cos128, sin128 = _rope_tables(pos)
    new_convs, new_states = [], []
    kf = vf = kb = vb = None
    for l in range(DEPTH):
        if l < DEPTH // 2:
            qkv_pre, z, ab = _linear(hb, wts["w_in"][l], (QKV_DIM, D_MODEL, LANES), (F32, BF16, F32))
            pre3 = qkv_pre.reshape(n_seq, seq_len, QKV_DIM)
            new_convs.append(pre3[:, t_real - (CONV_W - 1):t_real])
            if conv_state is None:
                prev = None
            else:
                prev = jnp.pad(conv_state[l], ((0, 0), (SAMPLE_PAD_T - (CONV_W - 1), 0), (0, 0)))
                prev = prev.reshape(n_seq * SAMPLE_PAD_T, QKV_DIM)
            q, k, v, gb = _conv_gate(qkv_pre, prev, ab, wts["w_conv"][l], wts["a_log"][l], wts["dt_bias"][l],
                                     seq_len=seq_len, t_real=t_real, chunk=DELTA_CHUNK)
            s0 = None if delta_state is None else delta_state[l]
            o, s_new = _delta_rule(q, k, v, gb, s0, n_seq=n_seq, seq_len=seq_len, chunk=DELTA_CHUNK)
            new_states.append(s_new)
            h, hb = _mix_out(o, z, h, wts["g_norm"][l], wts["w_out"][l], wts["ln_mix_g"][l], wts["ln_mix_b"][l])
        else:
            j = l - DEPTH // 2
            lam_init = 0.8 - 0.6 * math.exp(-0.3 * l)
            if past is None:
                q = _q_proj(hb, wts["w_q"][j], cos128, sin128, BF16)
                o = _attn_prompt(q, kb, vb, wts["lambda"][j], lam_init, n_seq=n_seq, seq_len=seq_len)
            else:
                q = _q_proj(hb, wts["w_q"][j], cos128, sin128, F32)
                o = _attn_sample(q, kf, vf, past[0], past[1], past[2], wts["lambda"][j], lam_init, t_real=t_real)
            h, hb = _mix_out(o, None, h, wts["g_sub"][j], wts["w_o"][j], wts["ln_mix_g"][l], wts["ln_mix_b"][l],
                             post_scale=1.0 - lam_init)
        if l % 2 == 0:
            i = l // 2
            h, hb = _dense_ffn(hb, h, wts["w_gate_d"][i], wts["w_up_d"][i], wts["w_down_d"][i],
                               wts["ln_ffn_g"][l], wts["ln_ffn_b"][l])
        else:
            i = l // 2
            h, hb = _moe_ffn(h, hb, wts["w_router"][i], wts["w_gate_e"][i], wts["w_up_e"][i], wts["w_down_e"][i],
                             wts["ln_ffn_g"][l], wts["ln_ffn_b"][l])
        if l == DEPTH // 2 - 1:
            kf, vf, kb, vb = _kv_proj(hb, wts["w_kv"], cos128, sin128)
    return h, jnp.stack(new_convs), jnp.stack(new_states), kf, vf


def kernel(x_prompt, x_sample, state_delta, state_conv, cache_k, cache_v, page_table, w_in_a, w_conv_a, a_log_a, dt_bias_a, g_norm_a, w_out_a, w_kv, w_q_b, lambda_b, g_sub_b, w_o_b, ln_mix_g, ln_mix_b, ln_ffn_g, ln_ffn_b, w_gate_d, w_up_d, w_down_d, w_router, w_gate_e, w_up_e, w_down_e):
    n_a = w_in_a.shape[0]
    bp, tp, _ = x_prompt.shape
    bs, ts, _ = x_sample.shape
    assert CONV_W - 1 <= ts <= SAMPLE_PAD_T and tp % SUBLANES == 0

    w_in = jnp.pad(w_in_a, ((0, 0), (0, 0), (0, LANES - 2 * N_HEADS))).astype(BF16)
    wts = dict(
        w_in=w_in, w_conv=w_conv_a, a_log=a_log_a, dt_bias=dt_bias_a, g_norm=g_norm_a,
        w_out=w_out_a.astype(BF16), w_kv=w_kv.astype(BF16), w_q=w_q_b.astype(BF16), g_sub=g_sub_b,
        w_o=w_o_b.astype(BF16), ln_mix_g=ln_mix_g, ln_mix_b=ln_mix_b, ln_ffn_g=ln_ffn_g, ln_ffn_b=ln_ffn_b,
        w_gate_d=w_gate_d.astype(BF16), w_up_d=w_up_d.astype(BF16), w_down_d=w_down_d.astype(BF16),
        w_router=w_router, w_gate_e=w_gate_e.astype(BF16), w_up_e=w_up_e.astype(BF16),
        w_down_e=w_down_e.astype(BF16))
    wts["lambda"] = lambda_b

    pos_p = jnp.tile(jnp.arange(tp), bp)
    y_p, conv_p, delta_p, k_p, v_p = _trunk(x_prompt.reshape(bp * tp, D_MODEL), pos_p, None, None, None, wts,
                                            n_seq=bp, seq_len=tp, t_real=tp)

    n_pages = page_table.shape[1]
    page = cache_k.shape[1]
    past_len = n_pages * page
    xs = jnp.pad(x_sample, ((0, 0), (0, SAMPLE_PAD_T - ts), (0, 0))).reshape(bs * SAMPLE_PAD_T, D_MODEL)
    pos_s = jnp.tile(past_len + jnp.arange(SAMPLE_PAD_T), bs)
    past = (cache_k.reshape(cache_k.shape[0], page, D_MODEL),
            cache_v.reshape(cache_v.shape[0], page * N_HEADS, HEAD_DIM), page_table)
    y_s, conv_s, delta_s, k_s, v_s = _trunk(xs, pos_s, state_conv, state_delta, past, wts,
                                            n_seq=bs, seq_len=SAMPLE_PAD_T, t_real=ts)

    def unpad(a):
        return a.reshape(bs, SAMPLE_PAD_T, D_MODEL)[:, :ts]

    return (y_p.reshape(bp, tp, D_MODEL), unpad(y_s), delta_p, conv_p,
            k_p.reshape(bp, tp, N_HEADS, 2, D_HEAD_B), v_p.reshape(bp, tp, N_HEADS, HEAD_DIM),
            delta_s, conv_s,
            unpad(k_s).reshape(bs, ts, N_HEADS, 2, D_HEAD_B), unpad(v_s).reshape(bs, ts, N_HEADS, HEAD_DIM))
```

```python
import functools
import math

import jax
import jax.numpy as jnp
from jax import lax
from jax.experimental import pallas as pl
from jax.experimental.pallas import tpu as pltpu

F32 = jnp.float32
BF16 = jnp.bfloat16
HIGHEST = lax.Precision.HIGHEST

LANES = 128
SUBLANES = 8
BF16_ROWS = 16
VMEM_LIMIT = 56 * 1024 * 1024

D_MODEL = 1024
N_HEADS = 8
HEAD_DIM = 128
QKV_DIM = 3 * D_MODEL
CONV_W = 4
DELTA_CHUNK = 64
D_HEAD_B = 64
ROPE_THETA = 10000.0
N_EXPERTS = 8
DEPTH = 4
DEEPNORM_ALPHA = (2.0 * DEPTH) ** 0.25
LN_EPS = 1e-5
RMS_EPS = 1e-6
SAMPLE_PAD_T = 8


def _cparams(semantics):
    return pltpu.CompilerParams(dimension_semantics=semantics, vmem_limit_bytes=VMEM_LIMIT)


def _dot(a, b, precision=None):
    return jnp.dot(a, b, preferred_element_type=F32, precision=precision)


def _dot_nt(a, b, precision=None):
    return lax.dot_general(a, b, (((1,), (1,)), ((), ())), preferred_element_type=F32, precision=precision)


def _dot_tn(a, b, precision=None):
    return lax.dot_general(a, b, (((0,), (0,)), ((), ())), preferred_element_type=F32, precision=precision)


def _layer_norm(y, g, b):
    mu = jnp.mean(y, axis=-1, keepdims=True)
    d = y - mu
    var = jnp.mean(d * d, axis=-1, keepdims=True)
    return d * lax.rsqrt(var + LN_EPS) * g + b


def _silu(x):
    return x * jax.nn.sigmoid(x)


def _resident(shape):
    return pl.BlockSpec(shape, lambda i: (0,) * len(shape), pipeline_mode=pl.Buffered(1))


def _row_tile(m, pref):
    t = min(pref, m)
    while m % t:
        t //= 2
    return t


def _linear_kernel(x_ref, w_ref, *out_refs, widths, tn):
    x = x_ref[...]
    col = 0
    for o_ref, width in zip(out_refs, widths):
        for c in range(0, width, tn):
            cw = min(tn, width - c)
            o_ref[:, c:c + cw] = _dot(x, w_ref[:, col + c:col + c + cw]).astype(o_ref.dtype)
        col += width


def _linear(x, w, widths, dtypes, tm=512, tn=512):
    m, k = x.shape
    tm = _row_tile(m, tm)
    n = sum(widths)
    return pl.pallas_call(
        functools.partial(_linear_kernel, widths=tuple(widths), tn=tn),
        grid=(m // tm,),
        in_specs=[pl.BlockSpec((tm, k), lambda i: (i, 0)),
                  _resident((k, n))],
        out_specs=[pl.BlockSpec((tm, wd), lambda i: (i, 0)) for wd in widths],
        out_shape=[jax.ShapeDtypeStruct((m, wd), dt) for wd, dt in zip(widths, dtypes)],
        compiler_params=_cparams(("parallel",)),
        name="linear",
    )(x, w)


def _rope(x, cos128, sin128):
    n = x.shape[1]
    reps = n // LANES
    cos_t = jnp.concatenate([cos128] * reps, axis=1)
    sin_t = jnp.concatenate([sin128] * reps, axis=1)
    lane = lax.broadcasted_iota(jnp.int32, x.shape, 1)
    first_half = (lane % D_HEAD_B) < (D_HEAD_B // 2)
    rot = jnp.where(first_half, pltpu.roll(x, n - D_HEAD_B // 2, 1), pltpu.roll(x, D_HEAD_B // 2, 1))
    return x * cos_t + rot * sin_t


def _q_proj_kernel(x_ref, w_ref, cos_ref, sin_ref, q_ref, *, scale):
    q = _dot(x_ref[...], w_ref[...])
    q_ref[...] = (_rope(q, cos_ref[...], sin_ref[...]) * scale).astype(q_ref.dtype)


def _q_proj(x, w, cos, sin, out_dtype, tm=512):
    m, k = x.shape
    tm = _row_tile(m, tm)
    return pl.pallas_call(
        functools.partial(_q_proj_kernel, scale=D_HEAD_B ** -0.5 * math.log2(math.e)),
        grid=(m // tm,),
        in_specs=[pl.BlockSpec((tm, k), lambda i: (i, 0)),
                  _resident((k, D_MODEL)),
                  pl.BlockSpec((tm, LANES), lambda i: (i, 0)),
                  pl.BlockSpec((tm, LANES), lambda i: (i, 0))],
        out_specs=pl.BlockSpec((tm, D_MODEL), lambda i: (i, 0)),
        out_shape=jax.ShapeDtypeStruct((m, D_MODEL), out_dtype),
        compiler_params=_cparams(("parallel",)),
        name="q_proj",
    )(x, w, cos, sin)


def _kv_proj_kernel(x_ref, w_ref, cos_ref, sin_ref, k_ref, v_ref, kb_ref, vb_ref):
    x = x_ref[...]
    k = _rope(_dot(x, w_ref[:, :D_MODEL]), cos_ref[...], sin_ref[...])
    v = _dot(x, w_ref[:, D_MODEL:])
    k_ref[...] = k
    v_ref[...] = v
    kb_ref[...] = k.astype(BF16)
    vb_ref[...] = v.astype(BF16)


def _kv_proj(x, w, cos, sin, tm=512):
    m, k = x.shape
    tm = _row_tile(m, tm)
    row = lambda i: (i, 0)
    return pl.pallas_call(
        _kv_proj_kernel,
        grid=(m // tm,),
        in_specs=[pl.BlockSpec((tm, k), row),
                  _resident((k, 2 * D_MODEL)),
                  pl.BlockSpec((tm, LANES), row),
                  pl.BlockSpec((tm, LANES), row)],
        out_specs=[pl.BlockSpec((tm, D_MODEL), row)] * 4,
        out_shape=[jax.ShapeDtypeStruct((m, D_MODEL), F32)] * 2 + [jax.ShapeDtypeStruct((m, D_MODEL), BF16)] * 2,
        compiler_params=_cparams(("parallel",)),
        name="kv_proj",
    )(x, w, cos, sin)


def _conv_gate_kernel(x_ref, p_ref, ab_ref, cw_ref, alog_ref, dtb_ref, q_ref, k_ref, v_ref, gb_ref,
                      *, seg, t_real, chunk, blocks_per_seq):
    tm = x_ref.shape[0]
    row = lax.broadcasted_iota(jnp.int32, (tm, 1), 0)
    tloc = row % seg if seg < tm else row
    valid = tloc < t_real
    if seg == tm:
        not_first = (pl.program_id(0) % blocks_per_seq) != 0
        row8 = lax.broadcasted_iota(jnp.int32, (SUBLANES, 1), 0)

    for c in range(QKV_DIM // LANES):
        cs = slice(c * LANES, (c + 1) * LANES)
        xc = x_ref[:, cs]
        acc = cw_ref[CONV_W - 1:CONV_W, cs] * xc
        for j in range(1, CONV_W):
            sh = pltpu.roll(xc, j, 0)
            if seg == tm:
                halo = jnp.where(not_first, p_ref[:, cs], 0.0)
                head = jnp.where(row8 < j, pltpu.roll(halo, j, 0), sh[:SUBLANES])
                sh = jnp.concatenate([head, sh[SUBLANES:]], axis=0)
            else:
                prev = pltpu.roll(p_ref[:, cs], tm + j - SUBLANES, 0)
                sh = jnp.where(tloc >= j, sh, prev)
            acc = acc + cw_ref[CONV_W - 1 - j:CONV_W - j, cs] * sh
        y = _silu(acc)
        if c < 2 * N_HEADS:
            y = y * lax.rsqrt(jnp.sum(y * y, axis=-1, keepdims=True) + 1e-6)
        if c < N_HEADS:
            y = y * (HEAD_DIM ** -0.5)
        y = jnp.where(valid, y, 0.0)
        if c < N_HEADS:
            q_ref[:, cs] = y
        elif c < 2 * N_HEADS:
            k_ref[:, (c - N_HEADS) * LANES:(c - N_HEADS + 1) * LANES] = y
        else:
            v_ref[:, (c - 2 * N_HEADS) * LANES:(c - 2 * N_HEADS + 1) * LANES] = y

    ab = ab_ref[...]
    z = ab + dtb_ref[...]
    softplus = jnp.maximum(z, 0.0) + jnp.log1p(jnp.exp(-jnp.abs(z)))
    g = jnp.where(valid, -jnp.exp(alog_ref[...]) * softplus, 0.0)
    beta = jnp.where(valid, jax.nn.sigmoid(ab), 0.0)
    ri = lax.broadcasted_iota(jnp.int32, (tm, tm), 0)
    ci = lax.broadcasted_iota(jnp.int32, (tm, tm), 1)
    tril = jnp.where((ri // chunk == ci // chunk) & (ci <= ri), 1.0, 0.0)
    g_cum = _dot(tril, g, precision=HIGHEST)
    lane = lax.broadcasted_iota(jnp.int32, (tm, LANES), 1)
    gb_ref[...] = jnp.where(lane < N_HEADS, g_cum, beta)


def _conv_gate(qkv_pre, prev, ab, conv_w, a_log, dt_bias, *, seq_len, t_real, chunk, tm=256):
    m = qkv_pre.shape[0]
    if prev is None:
        tm = _row_tile(seq_len, tm)
        seg = tm
        blocks_per_seq = seq_len // tm
        tiles_per_block = tm // SUBLANES
        p_arr = qkv_pre
        p_spec = pl.BlockSpec((SUBLANES, QKV_DIM), lambda i: (jnp.maximum(i * tiles_per_block - 1, 0), 0))
    else:
        tm = _row_tile(m, tm)
        seg = seq_len
        blocks_per_seq = 1
        p_arr = prev
        p_spec = pl.BlockSpec((tm, QKV_DIM), lambda i: (i, 0))
    assert seg % chunk == 0 or chunk % seg == 0
    row = lambda i: (i, 0)
    const = lambda i: (0, 0)
    cw = jnp.zeros((SUBLANES, QKV_DIM), F32).at[:CONV_W].set(conv_w)
    alog = jnp.zeros((1, LANES), F32).at[0, :N_HEADS].set(a_log)
    dtb = jnp.zeros((1, LANES), F32).at[0, :N_HEADS].set(dt_bias)
    return pl.pallas_call(
        functools.partial(_conv_gate_kernel, seg=seg, t_real=t_real, chunk=min(chunk, seg),
                          blocks_per_seq=blocks_per_seq),
        grid=(m // tm,),
        in_specs=[pl.BlockSpec((tm, QKV_DIM), row), p_spec, pl.BlockSpec((tm, LANES), row),
                  pl.BlockSpec((SUBLANES, QKV_DIM), const), pl.BlockSpec((1, LANES), const),
                  pl.BlockSpec((1, LANES), const)],
        out_specs=[pl.BlockSpec((tm, D_MODEL), row)] * 3 + [pl.BlockSpec((tm, LANES), row)],
        out_shape=[jax.ShapeDtypeStruct((m, D_MODEL), F32)] * 3 + [jax.ShapeDtypeStruct((m, LANES), F32)],
        compiler_params=_cparams(("parallel",)),
        name="conv_gate",
    )(qkv_pre, p_arr, ab, cw, alog, dtb)


def _delta_prepare(qs, ks, vs, gs, bs):
    n = len(qs)
    c = qs[0].shape[0]
    ri = lax.broadcasted_iota(jnp.int32, (c, c), 0)
    ci = lax.broadcasted_iota(jnp.int32, (c, c), 1)
    causal = ci <= ri
    strict = ci < ri
    eye = ci == ri
    decay, kb, k_b, pw = [], [], [], []
    for i in range(n):
        g_row = jnp.sum(jnp.where(eye, gs[i], 0.0), axis=0, keepdims=True)
        decay.append(jnp.where(causal, jnp.exp(jnp.where(causal, gs[i] - g_row, 0.0)), 0.0))
        kb.append(ks[i] * bs[i])
        k_b.append(ks[i].astype(BF16))
    for i in range(n):
        pw.append(-jnp.where(strict, _dot_nt(kb[i].astype(BF16), k_b[i]) * decay[i], 0.0))
    inv_m1 = list(pw)
    for _ in range(int(math.log2(c)) - 1):
        for i in range(n):
            pw_b = pw[i].astype(BF16)
            pw[i] = _dot(pw_b, pw_b)
        for i in range(n):
            inv_m1[i] = inv_m1[i] + pw[i] + _dot(inv_m1[i].astype(BF16), pw[i].astype(BF16))
    out = []
    for i in range(n):
        e_g = jnp.exp(gs[i])
        rhs = jnp.concatenate([vs[i] * bs[i], kb[i] * e_g], axis=1)
        sol = rhs + _dot(inv_m1[i].astype(BF16), rhs.astype(BF16))
        attn = _dot_nt(qs[i].astype(BF16), k_b[i]) * decay[i]
        g_last = gs[i][c - 1:c, :]
        out.append(dict(u=sol[:, :HEAD_DIM], w=sol[:, HEAD_DIM:].astype(BF16), attn=attn.astype(BF16),
                        q_dec=(qs[i] * e_g).astype(BF16),
                        k_dec=(ks[i] * jnp.exp(g_last - gs[i])).astype(BF16), decay_last=jnp.exp(g_last)))
    return out


def _delta_apply(prep, states):
    n = len(prep)
    sb = [s.astype(BF16) for s in states]
    v_new = [(prep[i]["u"] - _dot(prep[i]["w"], sb[i])).astype(BF16) for i in range(n)]
    outs = [_dot(prep[i]["q_dec"], sb[i]) + _dot(prep[i]["attn"], v_new[i]) for i in range(n)]
    new_states = [states[i] * prep[i]["decay_last"] + _dot_tn(prep[i]["k_dec"], v_new[i]) for i in range(n)]
    return outs, new_states


def _delta_kernel(*refs, chunk, has_s0):
    if has_s0:
        q_ref, k_ref, v_ref, gb_ref, s0_ref, o_ref, s_ref = refs
    else:
        q_ref, k_ref, v_ref, gb_ref, o_ref, s_ref = refs
        s0_ref = None
    rows = q_ref.shape[0]

    @pl.when(pl.program_id(1) == 0)
    def _init():
        if has_s0:
            s_ref[...] = s0_ref[...]
        else:
            s_ref[...] = jnp.zeros_like(s_ref)

    n_chunks = rows // chunk
    qs, ks, vs, gs, bs = [], [], [], [], []
    for ci in range(n_chunks):
        rs = slice(ci * chunk, (ci + 1) * chunk)
        gb = gb_ref[rs, :]
        for h in range(N_HEADS):
            hs = slice(h * HEAD_DIM, (h + 1) * HEAD_DIM)
            qs.append(q_ref[rs, hs])
            ks.append(k_ref[rs, hs])
            vs.append(v_ref[rs, hs])
            gs.append(gb[:, h:h + 1])
            bs.append(gb[:, N_HEADS + h:N_HEADS + h + 1])
    prep = _delta_prepare(qs, ks, vs, gs, bs)
    n_states = s_ref.shape[0]
    if n_states == 1:
        states = [s_ref[0, h] for h in range(N_HEADS)]
        outs = []
        for ci in range(n_chunks):
            out_c, states = _delta_apply(prep[ci * N_HEADS:(ci + 1) * N_HEADS], states)
            outs += out_c
    else:
        states = [s_ref[ci, h] for ci in range(n_chunks) for h in range(N_HEADS)]
        outs, states = _delta_apply(prep, states)
    for ci in range(n_chunks):
        for h in range(N_HEADS):
            o_ref[ci * chunk:(ci + 1) * chunk, h * HEAD_DIM:(h + 1) * HEAD_DIM] = outs[ci * N_HEADS + h]
    for si in range(n_states):
        for h in range(N_HEADS):
            s_ref[si, h] = states[si * N_HEADS + h]


def _delta_rule(q, k, v, gb, s0, *, n_seq, seq_len, chunk, rows_per_step=128, short_seqs_per_step=4):
    m = q.shape[0]
    chunk = min(chunk, seq_len)
    if seq_len == chunk:
        sps = _row_tile(n_seq, short_seqs_per_step)
        rows = sps * chunk
        grid = (n_seq // sps, 1)
        row = lambda s, i: (s, 0)
    else:
        sps = 1
        rows = _row_tile(seq_len, max(rows_per_step, chunk))
        steps = seq_len // rows
        grid = (n_seq, steps)
        row = lambda s, i: (s * steps + i, 0)
    st = lambda s, i: (s, 0, 0, 0)
    in_specs = [pl.BlockSpec((rows, D_MODEL), row)] * 3 + [pl.BlockSpec((rows, LANES), row)]
    args = [q, k, v, gb]
    if s0 is not None:
        in_specs.append(pl.BlockSpec((sps, N_HEADS, HEAD_DIM, HEAD_DIM), st))
        args.append(s0)
    return pl.pallas_call(
        functools.partial(_delta_kernel, chunk=chunk, has_s0=s0 is not None),
        grid=grid,
        in_specs=in_specs,
        out_specs=[pl.BlockSpec((rows, D_MODEL), row), pl.BlockSpec((sps, N_HEADS, HEAD_DIM, HEAD_DIM), st)],
        out_shape=[jax.ShapeDtypeStruct((m, D_MODEL), F32),
                   jax.ShapeDtypeStruct((n_seq, N_HEADS, HEAD_DIM, HEAD_DIM), F32)],
        compiler_params=_cparams(("parallel", "arbitrary")),
        name="delta_rule",
    )(*args)


def _mix_out_kernel(*refs, gated, post_scale):
    if gated:
        o_ref, z_ref, h_ref, gn_ref, w_ref, lg_ref, lb_ref, hf_ref, hb_ref = refs
    else:
        o_ref, h_ref, gn_ref, w_ref, lg_ref, lb_ref, hf_ref, hb_ref = refs
    parts = []
    for hd in range(N_HEADS):
        hs = slice(hd * HEAD_DIM, (hd + 1) * HEAD_DIM)
        o = o_ref[:, hs]
        y = o * lax.rsqrt(jnp.mean(o * o, axis=-1, keepdims=True) + RMS_EPS) * gn_ref[:, hs]
        if gated:
            y = y * _silu(z_ref[:, hs].astype(F32))
        else:
            y = y * post_scale
        parts.append(y.astype(BF16))
    mix = _dot(jnp.concatenate(parts, axis=1), w_ref[...])
    hn = _layer_norm(DEEPNORM_ALPHA * h_ref[...] + mix, lg_ref[...], lb_ref[...])
    hf_ref[...] = hn
    hb_ref[...] = hn.astype(BF16)


def _mix_out(o, z, h, g_norm, w, ln_g, ln_b, *, post_scale=1.0, tm=512):
    m = o.shape[0]
    tm = _row_tile(m, tm)
    row = lambda i: (i, 0)
    const = lambda i: (0, 0)
    gated = z is not None
    gn = jnp.tile(g_norm.reshape(1, HEAD_DIM), (1, N_HEADS))
    args = [o] + ([z] if gated else []) + [h, gn, w, ln_g.reshape(1, -1), ln_b.reshape(1, -1)]
    in_specs = ([pl.BlockSpec((tm, D_MODEL), row)] * (3 if gated else 2)
                + [pl.BlockSpec((1, D_MODEL), const), _resident((D_MODEL, D_MODEL)),
                   pl.BlockSpec((1, D_MODEL), const), pl.BlockSpec((1, D_MODEL), const)])
    return pl.pallas_call(
        functools.partial(_mix_out_kernel, gated=gated, post_scale=post_scale),
        grid=(m // tm,),
        in_specs=in_specs,
        out_specs=[pl.BlockSpec((tm, D_MODEL), row)] * 2,
        out_shape=[jax.ShapeDtypeStruct((m, D_MODEL), F32), jax.ShapeDtypeStruct((m, D_MODEL), BF16)],
        compiler_params=_cparams(("parallel",)),
        name="mix_out",
    )(*args)


def _dense_ffn_kernel(xb_ref, h_ref, wg_ref, wu_ref, wd_ref, lg_ref, lb_ref, hf_ref, hb_ref, *, tf):
    x = xb_ref[...]
    f_dim = wg_ref.shape[1]
    acc = DEEPNORM_ALPHA * h_ref[...]
    for c in range(0, f_dim, tf):
        cw = min(tf, f_dim - c)
        g = _dot(x, wg_ref[:, c:c + cw])
        u = _dot(x, wu_ref[:, c:c + cw])
        acc = acc + _dot((_silu(g) * u).astype(BF16), wd_ref[c:c + cw, :])
    hn = _layer_norm(acc, lg_ref[...], lb_ref[...])
    hf_ref[...] = hn
    hb_ref[...] = hn.astype(BF16)


def _dense_ffn(xb, h, wg, wu, wd, ln_g, ln_b, *, tm=512, tf=512):
    m = xb.shape[0]
    f_dim = wg.shape[1]
    tm = _row_tile(m, tm)
    row = lambda i: (i, 0)
    const = lambda i: (0, 0)
    return pl.pallas_call(
        functools.partial(_dense_ffn_kernel, tf=tf),
        grid=(m // tm,),
        in_specs=[pl.BlockSpec((tm, D_MODEL), row), pl.BlockSpec((tm, D_MODEL), row),
                  _resident((D_MODEL, f_dim)), _resident((D_MODEL, f_dim)), _resident((f_dim, D_MODEL)),
                  pl.BlockSpec((1, D_MODEL), const), pl.BlockSpec((1, D_MODEL), const)],
        out_specs=[pl.BlockSpec((tm, D_MODEL), row)] * 2,
        out_shape=[jax.ShapeDtypeStruct((m, D_MODEL), F32), jax.ShapeDtypeStruct((m, D_MODEL), BF16)],
        compiler_params=_cparams(("parallel",)),
        name="dense_ffn",
    )(xb, h, wg, wu, wd, ln_g.reshape(1, -1), ln_b.reshape(1, -1))


MOE_TM = 256
MOE_BM = 512
MOE_ALIGN = BF16_ROWS


def _router_kernel(h_ref, wr_ref, a_ref, r_ref, g_ref, cnt_ref):
    tm = h_ref.shape[0]
    logits = _dot_nt(wr_ref[...], h_ref[...], precision=HIGHEST)
    e_idx = lax.broadcasted_iota(jnp.int32, logits.shape, 0).astype(F32)
    m1 = jnp.max(logits, axis=0, keepdims=True)
    i1 = jnp.min(jnp.where(logits == m1, e_idx, float(N_EXPERTS)), axis=0, keepdims=True)
    first = e_idx == i1
    rest = jnp.where(first, -jnp.inf, logits)
    m2 = jnp.max(rest, axis=0, keepdims=True)
    i2 = jnp.min(jnp.where(rest == m2, e_idx, float(N_EXPERTS)), axis=0, keepdims=True)
    second = e_idx == i2
    ex = jnp.exp(m2 - m1)
    g1 = 1.0 / (1.0 + ex)
    g2 = ex / (1.0 + ex)
    routed = jnp.where(first | second, 1.0, 0.0)
    ji = lax.broadcasted_iota(jnp.int32, (tm, tm), 0)
    ii = lax.broadcasted_iota(jnp.int32, (tm, tm), 1)
    before = jnp.where(ji < ii, 1.0, 0.0).astype(BF16)
    a_ref[...] = routed
    r_ref[...] = _dot(routed.astype(BF16), before)
    g_ref[...] = jnp.where(first, g1, 0.0) + jnp.where(second, g2, 0.0)
    cnt_ref[0] = jnp.broadcast_to(jnp.sum(routed, axis=1, keepdims=True), (N_EXPERTS, LANES))


def _router(h, w_router_t, tm):
    m = h.shape[0]
    nb = m // tm
    col = lambda i: (0, i)
    return pl.pallas_call(
        _router_kernel,
        grid=(nb,),
        in_specs=[pl.BlockSpec((tm, D_MODEL), lambda i: (i, 0)),
                  pl.BlockSpec((N_EXPERTS, D_MODEL), lambda i: (0, 0))],
        out_specs=[pl.BlockSpec((N_EXPERTS, tm), col)] * 3
                  + [pl.BlockSpec((1, N_EXPERTS, LANES), lambda i: (i, 0, 0))],
        out_shape=[jax.ShapeDtypeStruct((N_EXPERTS, m), F32)] * 3
                  + [jax.ShapeDtypeStruct((nb, N_EXPERTS, LANES), F32)],
        compiler_params=_cparams(("parallel",)),
        name="moe_router",
    )(h, w_router_t)


def _dispatch_kernel(off_ref, xb_ref, a_ref, r_ref, rows_in_ref, rows_ref, slab_ref, sem_ref):
    del rows_in_ref
    b = pl.program_id(0)
    nb = pl.num_programs(0)
    slot = b % 2
    tm = xb_ref.shape[0]

    def slab_copy(which_slot, which_block, e):
        off = pl.multiple_of(off_ref[which_block * N_EXPERTS + e], MOE_ALIGN)
        return pltpu.make_async_copy(slab_ref.at[which_slot, e], rows_ref.at[pl.ds(off, tm)],
                                     sem_ref.at[which_slot, e])

    x = xb_ref[...]
    r_idx = lax.broadcasted_iota(jnp.int32, (tm, tm), 0).astype(F32)
    for e in range(N_EXPERTS):
        pick = jnp.where((r_idx == r_ref[e:e + 1, :]) & (a_ref[e:e + 1, :] > 0.0), 1.0, 0.0)
        slab_ref[slot, e] = _dot(pick.astype(BF16), x).astype(BF16)

    @pl.when(b > 0)
    def _previous_block_landed():
        for e in range(N_EXPERTS):
            slab_copy(1 - slot, b - 1, e).wait()

    for e in range(N_EXPERTS):
        slab_copy(slot, b, e).start()

    @pl.when(b == nb - 1)
    def _drain():
        for e in range(N_EXPERTS):
            slab_copy(slot, b, e).wait()


def _dispatch(xb, a_t, r_t, off, n_rows, tm):
    m = xb.shape[0]
    nb = m // tm
    rows_init = jnp.zeros((n_rows, D_MODEL), BF16)
    grid_spec = pltpu.PrefetchScalarGridSpec(
        num_scalar_prefetch=1,
        grid=(nb,),
        in_specs=[pl.BlockSpec((tm, D_MODEL), lambda b, off: (b, 0)),
                  pl.BlockSpec((N_EXPERTS, tm), lambda b, off: (0, b)),
                  pl.BlockSpec((N_EXPERTS, tm), lambda b, off: (0, b)),
                  pl.BlockSpec(memory_space=pl.ANY)],
        out_specs=pl.BlockSpec(memory_space=pl.ANY),
        scratch_shapes=[pltpu.VMEM((2, N_EXPERTS, tm, D_MODEL), BF16),
                        pltpu.SemaphoreType.DMA((2, N_EXPERTS))],
    )
    return pl.pallas_call(
        _dispatch_kernel,
        grid_spec=grid_spec,
        out_shape=jax.ShapeDtypeStruct((n_rows, D_MODEL), BF16),
        input_output_aliases={4: 0},
        compiler_params=_cparams(("arbitrary",)),
        name="moe_dispatch",
    )(off, xb, a_t, r_t, rows_init)


def _expert_ffn_kernel(be_ref, valid_ref, x_ref, wg_ref, wu_ref, wd_ref, y_ref, acc_ref):
    del be_ref
    g_idx = pl.program_id(0)
    f = pl.program_id(1)
    nf = pl.num_programs(1)
    is_valid = valid_ref[g_idx] > 0

    @pl.when(is_valid)
    def _compute():
        x = x_ref[...]
        g = _dot(x, wg_ref[0])
        u = _dot(x, wu_ref[0])
        part = _dot((_silu(g) * u).astype(BF16), wd_ref[0])

        @pl.when(f == 0)
        def _():
            acc_ref[...] = part

        @pl.when(f > 0)
        def _():
            acc_ref[...] += part

        @pl.when(f == nf - 1)
        def _():
            y_ref[...] = acc_ref[...].astype(y_ref.dtype)

    @pl.when(jnp.logical_not(is_valid) & (f == nf - 1))
    def _skip():
        y_ref[...] = jnp.zeros_like(y_ref)


def _expert_ffn(x_rows, blk_e, valid, wg, wu, wd, bm, tf):
    n_rows = x_rows.shape[0]
    f_dim = wg.shape[2]
    nf = f_dim // tf
    grid_spec = pltpu.PrefetchScalarGridSpec(
        num_scalar_prefetch=2,
        grid=(n_rows // bm, nf),
        in_specs=[pl.BlockSpec((bm, D_MODEL), lambda g, f, be, va: (g, 0)),
                  pl.BlockSpec((1, D_MODEL, tf), lambda g, f, be, va: (be[g], 0, f)),
                  pl.BlockSpec((1, D_MODEL, tf), lambda g, f, be, va: (be[g], 0, f)),
                  pl.BlockSpec((1, tf, D_MODEL), lambda g, f, be, va: (be[g], f, 0))],
        out_specs=pl.BlockSpec((bm, D_MODEL), lambda g, f, be, va: (g, 0)),
        scratch_shapes=[pltpu.VMEM((bm, D_MODEL), F32)],
    )
    return pl.pallas_call(
        _expert_ffn_kernel,
        grid_spec=grid_spec,
        out_shape=jax.ShapeDtypeStruct((n_rows, D_MODEL), BF16),
        compiler_params=_cparams(("parallel", "arbitrary")),
        name="moe_expert_ffn",
    )(blk_e, valid, x_rows, wg, wu, wd)


def _combine_kernel(off_ref, h_ref, a_ref, r_ref, g_ref, lg_ref, lb_ref, rows_ref, hf_ref, hb_ref,
                    slab_ref, sem_ref, acc_ref):
    b = pl.program_id(0)
    nb = pl.num_programs(0)
    slot = b % 2
    tm = h_ref.shape[0]

    def slab_copy(which_slot, which_block, e):
        off = pl.multiple_of(off_ref[which_block * N_EXPERTS + e], MOE_ALIGN)
        return pltpu.make_async_copy(rows_ref.at[pl.ds(off, tm)], slab_ref.at[which_slot, e],
                                     sem_ref.at[which_slot, e])

    @pl.when(b == 0)
    def _prime():
        for e in range(N_EXPERTS):
            slab_copy(0, 0, e).start()

    @pl.when(b + 1 < nb)
    def _prefetch():
        for e in range(N_EXPERTS):
            slab_copy(1 - slot, b + 1, e).start()

    acc_ref[...] = DEEPNORM_ALPHA * h_ref[...]
    r_idx = lax.broadcasted_iota(jnp.int32, (tm, tm), 1).astype(F32)
    for e in range(N_EXPERTS):
        pick = jnp.where((r_idx == r_ref[:, e:e + 1]) & (a_ref[:, e:e + 1] > 0.0), 1.0, 0.0)
        slab_copy(slot, b, e).wait()
        acc_ref[...] += g_ref[:, e:e + 1] * _dot(pick.astype(BF16), slab_ref[slot, e])
    hn = _layer_norm(acc_ref[...], lg_ref[...], lb_ref[...])
    hf_ref[...] = hn
    hb_ref[...] = hn.astype(BF16)


def _combine(h, a_c, r_c, g_c, ln_g, ln_b, y_rows, off, tm):
    m = h.shape[0]
    nb = m // tm
    blk = lambda b, off: (b, 0)
    const = lambda b, off: (0, 0)
    grid_spec = pltpu.PrefetchScalarGridSpec(
        num_scalar_prefetch=1,
        grid=(nb,),
        in_specs=[pl.BlockSpec((tm, D_MODEL), blk),
                  pl.BlockSpec((tm, N_EXPERTS), blk), pl.BlockSpec((tm, N_EXPERTS), blk),
                  pl.BlockSpec((tm, N_EXPERTS), blk),
                  pl.BlockSpec((1, D_MODEL), const), pl.BlockSpec((1, D_MODEL), const),
                  pl.BlockSpec(memory_space=pl.ANY)],
        out_specs=[pl.BlockSpec((tm, D_MODEL), blk)] * 2,
        scratch_shapes=[pltpu.VMEM((2, N_EXPERTS, tm, D_MODEL), BF16), pltpu.SemaphoreType.DMA((2, N_EXPERTS)),
                        pltpu.VMEM((tm, D_MODEL), F32)],
    )
    return pl.pallas_call(
        _combine_kernel,
        grid_spec=grid_spec,
        out_shape=[jax.ShapeDtypeStruct((m, D_MODEL), F32), jax.ShapeDtypeStruct((m, D_MODEL), BF16)],
        compiler_params=_cparams(("arbitrary",)),
        name="moe_combine",
    )(off, h, a_c, r_c, g_c, ln_g.reshape(1, -1), ln_b.reshape(1, -1), y_rows)


def _moe_ffn(h, hb, w_router, wg, wu, wd, ln_g, ln_b):
    m = h.shape[0]
    tm = _row_tile(m, MOE_TM)
    nb = m // tm
    bm = MOE_BM if m >= N_EXPERTS * MOE_BM else tm
    f_dim = wg.shape[2]
    tf = f_dim // 2 if (f_dim // 2) % 256 == 0 else f_dim
    a_t, r_t, g_t, cnts = _router(h, w_router.T, tm)

    cnt = cnts[:, :, 0].astype(jnp.int32)
    padded = (cnt + MOE_ALIGN - 1) // MOE_ALIGN * MOE_ALIGN
    base = jnp.cumsum(padded, axis=0) - padded
    used = jnp.sum(padded, axis=0)
    cap = (used + tm + bm - 1) // bm * bm
    region = jnp.cumsum(cap) - cap
    off = (region[None, :] + base).reshape(-1).astype(jnp.int32)
    max_rows = 2 * m + nb * N_EXPERTS * (MOE_ALIGN - 1) + N_EXPERTS * (tm + bm)
    n_rows = (max_rows + bm - 1) // bm * bm
    g_idx = jnp.arange(n_rows // bm, dtype=jnp.int32)
    region_blk = region // bm
    blk_e = (jnp.sum(g_idx[:, None] >= region_blk[None, :], axis=1) - 1).astype(jnp.int32)
    n_blk = (used + bm - 1) // bm
    valid = ((g_idx - region_blk[blk_e]) < n_blk[blk_e]).astype(jnp.int32)

    x_rows = _dispatch(hb, a_t, r_t, off, n_rows, tm)
    y_rows = _expert_ffn(x_rows, blk_e, valid, wg, wu, wd, bm, tf)
    return _combine(h, a_t.T, r_t.T, g_t.T, ln_g, ln_b, y_rows, off, tm)


def _lambda_value(lam_ref, lam_init):
    s1 = jnp.sum(lam_ref[0:1, :] * lam_ref[1:2, :], axis=1, keepdims=True)
    s2 = jnp.sum(lam_ref[2:3, :] * lam_ref[3:4, :], axis=1, keepdims=True)
    return jnp.exp(s1) - jnp.exp(s2) + lam_init


def _attn_prompt_kernel(qi_ref, kj_ref, lam_ref, q_ref, k_ref, v_ref, o_ref, qq_ref, m_ref, acc_ref,
                        *, lam_init):
    p = pl.program_id(2)
    i = qi_ref[p]
    j = kj_ref[p]
    tq = q_ref.shape[0]
    tk = k_ref.shape[0]

    @pl.when(j == 0)
    def _init():
        q = q_ref[...]
        lane = lax.broadcasted_iota(jnp.int32, q.shape, 1)
        zero = jnp.zeros_like(q)
        qq_ref[0:tq, :] = jnp.where(lane < D_HEAD_B, q, zero)
        qq_ref[tq:2 * tq, :] = jnp.where(lane >= D_HEAD_B, q, zero)
        m_ref[...] = jnp.full_like(m_ref, -jnp.inf)
        acc_ref[...] = jnp.zeros_like(acc_ref)

    def step(masked):
        s = _dot_nt(qq_ref[...], k_ref[...])
        if masked:
            qpos = lax.broadcasted_iota(jnp.int32, s.shape, 0) % tq
            kpos = lax.broadcasted_iota(jnp.int32, s.shape, 1)
            s = jnp.where(kpos <= qpos, s, -jnp.inf)
        m_prev = m_ref[...]
        m_new = jnp.maximum(m_prev, jnp.max(s, axis=1, keepdims=True))
        alpha = jnp.exp2(m_prev - m_new)
        pr = jnp.exp2(s - jnp.concatenate([m_new] * (tk // LANES), axis=1))
        acc_ref[:, HEAD_DIM:] = alpha * acc_ref[:, HEAD_DIM:] + jnp.sum(pr, axis=1, keepdims=True)
        acc_ref[:, :HEAD_DIM] = alpha * acc_ref[:, :HEAD_DIM] + _dot(pr.astype(BF16), v_ref[...])
        m_ref[...] = m_new

    @pl.when(j < i)
    def _full():
        step(False)

    @pl.when(j == i)
    def _diag():
        step(True)
        lam = _lambda_value(lam_ref, lam_init)
        o = acc_ref[:, :HEAD_DIM] / acc_ref[:, HEAD_DIM:]
        o_ref[...] = o[:tq] - lam * o[tq:]


def _attn_prompt(q, k, v, lam_p, lam_init, *, n_seq, seq_len, tq=512):
    m = q.shape[0]
    tq = _row_tile(seq_len, tq)
    nq = seq_len // tq
    pairs = [(i, j) for i in range(nq) for j in range(i + 1)]
    qi = jnp.asarray([p[0] for p in pairs], jnp.int32)
    kj = jnp.asarray([p[1] for p in pairs], jnp.int32)
    lam_pad = jnp.zeros((SUBLANES, LANES), F32).at[:4, :D_HEAD_B].set(lam_p)
    grid_spec = pltpu.PrefetchScalarGridSpec(
        num_scalar_prefetch=2,
        grid=(n_seq, N_HEADS, len(pairs)),
        in_specs=[pl.BlockSpec((SUBLANES, LANES), lambda b, h, p, qi, kj: (0, 0)),
                  pl.BlockSpec((tq, HEAD_DIM), lambda b, h, p, qi, kj: (b * nq + qi[p], h)),
                  pl.BlockSpec((tq, HEAD_DIM), lambda b, h, p, qi, kj: (b * nq + kj[p], h)),
                  pl.BlockSpec((tq, HEAD_DIM), lambda b, h, p, qi, kj: (b * nq + kj[p], h))],
        out_specs=pl.BlockSpec((tq, HEAD_DIM), lambda b, h, p, qi, kj: (b * nq + qi[p], h)),
        scratch_shapes=[pltpu.VMEM((2 * tq, HEAD_DIM), BF16), pltpu.VMEM((2 * tq, LANES), F32),
                        pltpu.VMEM((2 * tq, HEAD_DIM + LANES), F32)],
    )
    assert tq % LANES == 0
    return pl.pallas_call(
        functools.partial(_attn_prompt_kernel, lam_init=lam_init),
        grid_spec=grid_spec,
        out_shape=jax.ShapeDtypeStruct((m, D_MODEL), F32),
        compiler_params=_cparams(("parallel", "parallel", "arbitrary")),
        name="attn_prompt",
    )(qi, kj, lam_pad, q, k, v)


def _attn_sample_kernel(*refs, pages_per_step, t_real, lam_init):
    pp = pages_per_step
    pt_ref, lam_ref, q_ref, kn_ref, vn_ref = refs[:5]
    k_refs = refs[5:5 + pp]
    v_refs = refs[5 + pp:5 + 2 * pp]
    o_ref, qbd_ref, spread_ref, m_ref, l_ref, acc_ref = refs[5 + 2 * pp:]
    del pt_ref
    j = pl.program_id(1)
    n_rows = qbd_ref.shape[0]
    page = k_refs[0].shape[1]
    rows_per_head = 2 * SAMPLE_PAD_T

    def online_update(s_list, pv_fn):
        m_prev = m_ref[...]
        m_cur = jnp.max(s_list[0], axis=1, keepdims=True)
        for s in s_list[1:]:
            m_cur = jnp.maximum(m_cur, jnp.max(s, axis=1, keepdims=True))
        m_new = jnp.maximum(m_prev, m_cur)
        alpha = jnp.exp2(m_prev - m_new)
        m_wide = jnp.concatenate([m_new] * (page // LANES), axis=1)
        prs = [jnp.exp2(s - m_wide) for s in s_list]
        l_add = jnp.sum(prs[0], axis=1, keepdims=True)
        for pr in prs[1:]:
            l_add = l_add + jnp.sum(pr, axis=1, keepdims=True)
        l_ref[...] = alpha * l_ref[...] + l_add
        acc_ref[...] = alpha * acc_ref[...] + pv_fn([pr.astype(BF16) for pr in prs])
        m_ref[...] = m_new

    @pl.when(j == 0)
    def _init():
        rep = jnp.concatenate([q_ref[...]] * (n_rows // SAMPLE_PAD_T), axis=0)
        r_grp = lax.broadcasted_iota(jnp.int32, rep.shape, 0) // SAMPLE_PAD_T
        l_grp = lax.broadcasted_iota(jnp.int32, rep.shape, 1) // D_HEAD_B
        qbd_ref[...] = jnp.where(r_grp == l_grp, rep, 0.0).astype(BF16)
        key = lax.broadcasted_iota(jnp.int32, spread_ref.shape, 0)
        col = lax.broadcasted_iota(jnp.int32, spread_ref.shape, 1)
        spread_ref[...] = jnp.where(col // N_HEADS == key, 1.0, 0.0).astype(BF16)
        m_ref[...] = jnp.full_like(m_ref, -jnp.inf)
        l_ref[...] = jnp.zeros_like(l_ref)
        acc_ref[...] = jnp.zeros_like(acc_ref)
        pad = jnp.zeros((page - SAMPLE_PAD_T, D_MODEL), F32)
        k_pad = jnp.concatenate([kn_ref[...], pad], axis=0).astype(BF16)
        v_pad = jnp.concatenate([vn_ref[...], pad], axis=0).astype(BF16)
        s = _dot_nt(qbd_ref[...], k_pad)
        t_q = lax.broadcasted_iota(jnp.int32, s.shape, 0) % SAMPLE_PAD_T
        t_k = lax.broadcasted_iota(jnp.int32, s.shape, 1)
        s = jnp.where((t_k <= t_q) & (t_k < t_real), s, -jnp.inf)

        def pv_new(prs):
            return jnp.concatenate(
                [_dot(prs[0][h * rows_per_head:(h + 1) * rows_per_head],
                      v_pad[:, h * HEAD_DIM:(h + 1) * HEAD_DIM]) for h in range(N_HEADS)], axis=0)

        online_update([s], pv_new)

    def pv_pages(prs):
        row_head = lax.broadcasted_iota(jnp.int32, (n_rows, page * N_HEADS), 0) // rows_per_head
        col_head = lax.broadcasted_iota(jnp.int32, (n_rows, page * N_HEADS), 1) % N_HEADS
        own_head = row_head == col_head
        total = None
        for p_i in range(pp):
            wide = jnp.where(own_head, _dot(prs[p_i], spread_ref[...]), 0.0).astype(BF16)
            part = _dot(wide, v_refs[p_i][0].astype(BF16))
            total = part if total is None else total + part
        return total

    qbd = qbd_ref[...]
    online_update([_dot_nt(qbd, k_refs[p_i][0].astype(BF16)) for p_i in range(pp)], pv_pages)

    @pl.when(j == pl.num_programs(1) - 1)
    def _finish():
        lam = _lambda_value(lam_ref, lam_init)
        o = acc_ref[...] / l_ref[...]
        for h in range(N_HEADS):
            r1 = h * rows_per_head
            r2 = r1 + SAMPLE_PAD_T
            o_ref[:, h * HEAD_DIM:(h + 1) * HEAD_DIM] = o[r1:r1 + SAMPLE_PAD_T] - lam * o[r2:r2 + SAMPLE_PAD_T]


def _attn_sample(q, k_new, v_new, cache_k, cache_v, page_table, lam_p, lam_init, *, t_real, pages_per_step=8):
    m = q.shape[0]
    n_seq, n_pages = page_table.shape
    page = cache_k.shape[1]
    pp = pages_per_step
    while n_pages % pp:
        pp //= 2
    lam_pad = jnp.zeros((SUBLANES, LANES), F32).at[:4, :D_HEAD_B].set(lam_p)
    n_rows = 2 * N_HEADS * SAMPLE_PAD_T
    seq = lambda s, j, pt: (s, 0)

    def page_spec(p_i, rows, width):
        return pl.BlockSpec((1, rows, width), lambda s, j, pt: (pt[s * n_pages + j * pp + p_i], 0, 0))

    grid_spec = pltpu.PrefetchScalarGridSpec(
        num_scalar_prefetch=1,
        grid=(n_seq, n_pages // pp),
        in_specs=[pl.BlockSpec((SUBLANES, LANES), lambda s, j, pt: (0, 0)),
                  pl.BlockSpec((SAMPLE_PAD_T, D_MODEL), seq), pl.BlockSpec((SAMPLE_PAD_T, D_MODEL), seq),
                  pl.BlockSpec((SAMPLE_PAD_T, D_MODEL), seq)]
                 + [page_spec(p_i, page, D_MODEL) for p_i in range(pp)]
                 + [page_spec(p_i, page * N_HEADS, HEAD_DIM) for p_i in range(pp)],
        out_specs=pl.BlockSpec((SAMPLE_PAD_T, D_MODEL), seq),
        scratch_shapes=[pltpu.VMEM((n_rows, D_MODEL), BF16), pltpu.VMEM((page, page * N_HEADS), BF16),
                        pltpu.VMEM((n_rows, LANES), F32), pltpu.VMEM((n_rows, LANES), F32),
                        pltpu.VMEM((n_rows, HEAD_DIM), F32)],
    )
    assert page == LANES and n_rows == LANES
    return pl.pallas_call(
        functools.partial(_attn_sample_kernel, pages_per_step=pp, t_real=t_real, lam_init=lam_init),
        grid_spec=grid_spec,
        out_shape=jax.ShapeDtypeStruct((m, D_MODEL), F32),
        compiler_params=_cparams(("parallel", "arbitrary")),
        name="attn_sample",
    )(page_table.reshape(-1), lam_pad, q, k_new, v_new, *([cache_k] * pp), *([cache_v] * pp))


def _rope_tables(pos):
    half = D_HEAD_B // 2
    inv_freq = ROPE_THETA ** (-jnp.arange(half, dtype=F32) / half)
    ang = pos.astype(F32)[:, None] * inv_freq[None, :]
    cos, sin = jnp.cos(ang), jnp.sin(ang)
    cos128 = jnp.tile(cos, (1, LANES // half))
    sin128 = jnp.tile(jnp.concatenate([-sin, sin], axis=1), (1, LANES // D_HEAD_B))
    return cos128, sin128


def _trunk(x, pos, conv_state, delta_state, past, wts, *, n_seq, seq_len, t_real):
    h = x
    hb = x.astype(BF16)
    cos128, sin128 = _rope_tables(pos)
    new_convs, new_states = [], []
    kf = vf = kb = vb = None
    for l in range(DEPTH):
        if l < DEPTH // 2:
            qkv_pre, z, ab = _linear(hb, wts["w_in"][l], (QKV_DIM, D_MODEL, LANES), (F32, BF16, F32))
            pre3 = qkv_pre.reshape(n_seq, seq_len, QKV_DIM)
            new_convs.append(pre3[:, t_real - (CONV_W - 1):t_real])
            if conv_state is None:
                prev = None
            else:
                prev = jnp.pad(conv_state[l], ((0, 0), (SAMPLE_PAD_T - (CONV_W - 1), 0), (0, 0)))
                prev = prev.reshape(n_seq * SAMPLE_PAD_T, QKV_DIM)
            q, k, v, gb = _conv_gate(qkv_pre, prev, ab, wts["w_conv"][l], wts["a_log"][l], wts["dt_bias"][l],
                                     seq_len=seq_len, t_real=t_real, chunk=DELTA_CHUNK)
            s0 = None if delta_state is None else delta_state[l]
            o, s_new = _delta_rule(q, k, v, gb, s0, n_seq=n_seq, seq_len=seq_len, chunk=DELTA_CHUNK)
            new_states.append(s_new)
            h, hb = _mix_out(o, z, h, wts["g_norm"][l], wts["w_out"][l], wts["ln_mix_g"][l], wts["ln_mix_b"][l])
        else:
            j = l - DEPTH // 2
            lam_init = 0.8 - 0.6 * math.exp(-0.3 * l)
            if past is None:
                q = _q_proj(hb, wts["w_q"][j], cos128, sin128, BF16)
                o = _attn_prompt(q, kb, vb, wts["lambda"][j], lam_init, n_seq=n_seq, seq_len=seq_len)
            else:
                q = _q_proj(hb, wts["w_q"][j], cos128, sin128, F32)
                o = _attn_sample(q, kf, vf, past[0], past[1], past[2], wts["lambda"][j], lam_init, t_real=t_real)
            h, hb = _mix_out(o, None, h, wts["g_sub"][j], wts["w_o"][j], wts["ln_mix_g"][l], wts["ln_mix_b"][l],
                             post_scale=1.0 - lam_init)
        if l % 2 == 0:
            i = l // 2
            h, hb = _dense_ffn(hb, h, wts["w_gate_d"][i], wts["w_up_d"][i], wts["w_down_d"][i],
                               wts["ln_ffn_g"][l], wts["ln_ffn_b"][l])
        else:
            i = l // 2
            h, hb = _moe_ffn(h, hb, wts["w_router"][i], wts["w_gate_e"][i], wts["w_up_e"][i], wts["w_down_e"][i],
                             wts["ln_ffn_g"][l], wts["ln_ffn_b"][l])
        if l == DEPTH // 2 - 1:
            kf, vf, kb, vb = _kv_proj(hb, wts["w_kv"], cos128, sin128)
    return h, jnp.stack(new_convs), jnp.stack(new_states), kf, vf


def kernel(x_prompt, x_sample, state_delta, state_conv, cache_k, cache_v, page_table, w_in_a, w_conv_a, a_log_a, dt_bias_a, g_norm_a, w_out_a, w_kv, w_q_b, lambda_b, g_sub_b, w_o_b, ln_mix_g, ln_mix_b, ln_ffn_g, ln_ffn_b, w_gate_d, w_up_d, w_down_d, w_router, w_gate_e, w_up_e, w_down_e):
    n_a = w_in_a.shape[0]
    bp, tp, _ = x_prompt.shape
    bs, ts, _ = x_sample.shape
    assert CONV_W - 1 <= ts <= SAMPLE_PAD_T and tp % SUBLANES == 0

    w_in = jnp.pad(w_in_a, ((0, 0), (0, 0), (0, LANES - 2 * N_HEADS))).astype(BF16)
    wts = dict(
        w_in=w_in, w_conv=w_conv_a, a_log=a_log_a, dt_bias=dt_bias_a, g_norm=g_norm_a,
        w_out=w_out_a.astype(BF16), w_kv=w_kv.astype(BF16), w_q=w_q_b.astype(BF16), g_sub=g_sub_b,
        w_o=w_o_b.astype(BF16), ln_mix_g=ln_mix_g, ln_mix_b=ln_mix_b, ln_ffn_g=ln_ffn_g, ln_ffn_b=ln_ffn_b,
        w_gate_d=w_gate_d.astype(BF16), w_up_d=w_up_d.astype(BF16), w_down_d=w_down_d.astype(BF16),
        w_router=w_router, w_gate_e=w_gate_e.astype(BF16), w_up_e=w_up_e.astype(BF16),
        w_down_e=w_down_e.astype(BF16))
    wts["lambda"] = lambda_b

    pos_p = jnp.tile(jnp.arange(tp), bp)
    y_p, conv_p, delta_p, k_p, v_p = _trunk(x_prompt.reshape(bp * tp, D_MODEL), pos_p, None, None, None, wts,
                                            n_seq=bp, seq_len=tp, t_real=tp)

    n_pages = page_table.shape[1]
    page = cache_k.shape[1]
    past_len = n_pages * page
    xs = jnp.pad(x_sample, ((0, 0), (0, SAMPLE_PAD_T - ts), (0, 0))).reshape(bs * SAMPLE_PAD_T, D_MODEL)
    pos_s = jnp.tile(past_len + jnp.arange(SAMPLE_PAD_T), bs)
    past = (cache_k.reshape(cache_k.shape[0], page, D_MODEL),
            cache_v.reshape(cache_v.shape[0], page * N_HEADS, HEAD_DIM), page_table)
    y_s, conv_s, delta_s, k_s, v_s = _trunk(xs, pos_s, state_conv, state_delta, past, wts,
                                            n_seq=bs, seq_len=SAMPLE_PAD_T, t_real=ts)

    def unpad(a):
        return a.reshape(bs, SAMPLE_PAD_T, D_MODEL)[:, :ts]

    return (y_p.reshape(bp, tp, D_MODEL), unpad(y_s), delta_p, conv_p,
            k_p.reshape(bp, tp, N_HEADS, 2, D_HEAD_B), v_p.reshape(bp, tp, N_HEADS, HEAD_DIM),
            delta_s, conv_s,
            unpad(k_s).reshape(bs, ts, N_HEADS, 2, D_HEAD_B), unpad(v_s).reshape(bs, ts, N_HEADS, HEAD_DIM))
```

```python
import functools
import math

import jax
import jax.numpy as jnp
from jax import lax
from jax.experimental import pallas as pl
from jax.experimental.pallas import tpu as pltpu

F32 = jnp.float32
BF16 = jnp.bfloat16
HIGHEST = lax.Precision.HIGHEST

LANES = 128
SUBLANES = 8
BF16_ROWS = 16
MXU_DEPTH = 256
VMEM_LIMIT = 56 * 1024 * 1024

D_MODEL = 1024
N_HEADS = 8
HEAD_DIM = 128
QKV_DIM = 3 * D_MODEL
CONV_W = 4
DELTA_CHUNK = 64
D_HEAD_B = 64
ROPE_THETA = 10000.0
N_EXPERTS = 8
DEPTH = 4
DEEPNORM_ALPHA = (2.0 * DEPTH) ** 0.25
LN_EPS = 1e-5
RMS_EPS = 1e-6
SAMPLE_PAD_T = 8


def _cparams(semantics):
    return pltpu.CompilerParams(dimension_semantics=semantics, vmem_limit_bytes=VMEM_LIMIT)


def _dot(a, b, precision=None):
    return jnp.dot(a, b, preferred_element_type=F32, precision=precision)


def _dot_nt(a, b, precision=None):
    return lax.dot_general(a, b, (((1,), (1,)), ((), ())), preferred_element_type=F32, precision=precision)


def _dot_tn(a, b, precision=None):
    return lax.dot_general(a, b, (((0,), (0,)), ((), ())), preferred_element_type=F32, precision=precision)


def _layer_norm(y, g, b):
    mu = jnp.mean(y, axis=-1, keepdims=True)
    d = y - mu
    var = jnp.mean(d * d, axis=-1, keepdims=True)
    return d * lax.rsqrt(var + LN_EPS) * g + b


def _silu(x):
    return x * jax.nn.sigmoid(x)


def _resident(shape):
    return pl.BlockSpec(shape, lambda i: (0,) * len(shape), pipeline_mode=pl.Buffered(1))


def _row_tile(m, pref):
    t = min(pref, m)
    while m % t:
        t //= 2
    return t


def _linear_kernel(x_ref, w_ref, *out_refs, widths, tn):
    x = x_ref[...]
    col = 0
    for o_ref, width in zip(out_refs, widths):
        for c in range(0, width, tn):
            cw = min(tn, width - c)
            o_ref[:, c:c + cw] = _dot(x, w_ref[:, col + c:col + c + cw]).astype(o_ref.dtype)
        col += width


def _linear(x, w, widths, dtypes, tm=512, tn=512):
    m, k = x.shape
    tm = _row_tile(m, tm)
    n = sum(widths)
    return pl.pallas_call(
        functools.partial(_linear_kernel, widths=tuple(widths), tn=tn),
        grid=(m // tm,),
        in_specs=[pl.BlockSpec((tm, k), lambda i: (i, 0)),
                  _resident((k, n))],
        out_specs=[pl.BlockSpec((tm, wd), lambda i: (i, 0)) for wd in widths],
        out_shape=[jax.ShapeDtypeStruct((m, wd), dt) for wd, dt in zip(widths, dtypes)],
        compiler_params=_cparams(("parallel",)),
        name="linear",
    )(x, w)


def _rope(x, cos128, sin128):
    n = x.shape[1]
    reps = n // LANES
    cos_t = jnp.concatenate([cos128] * reps, axis=1)
    sin_t = jnp.concatenate([sin128] * reps, axis=1)
    lane = lax.broadcasted_iota(jnp.int32, x.shape, 1)
    first_half = (lane % D_HEAD_B) < (D_HEAD_B // 2)
    rot = jnp.where(first_half, pltpu.roll(x, n - D_HEAD_B // 2, 1), pltpu.roll(x, D_HEAD_B // 2, 1))
    return x * cos_t + rot * sin_t


def _q_proj_kernel(x_ref, w_ref, cos_ref, sin_ref, q_ref, *, scale):
    q = _dot(x_ref[...], w_ref[...])
    q_ref[...] = (_rope(q, cos_ref[...], sin_ref[...]) * scale).astype(q_ref.dtype)


def _q_proj(x, w, cos, sin, out_dtype, tm=512):
    m, k = x.shape
    tm = _row_tile(m, tm)
    return pl.pallas_call(
        functools.partial(_q_proj_kernel, scale=D_HEAD_B ** -0.5 * math.log2(math.e)),
        grid=(m // tm,),
        in_specs=[pl.BlockSpec((tm, k), lambda i: (i, 0)),
                  _resident((k, D_MODEL)),
                  pl.BlockSpec((tm, LANES), lambda i: (i, 0)),
                  pl.BlockSpec((tm, LANES), lambda i: (i, 0))],
        out_specs=pl.BlockSpec((tm, D_MODEL), lambda i: (i, 0)),
        out_shape=jax.ShapeDtypeStruct((m, D_MODEL), out_dtype),
        compiler_params=_cparams(("parallel",)),
        name="q_proj",
    )(x, w, cos, sin)


def _kv_proj_kernel(x_ref, w_ref, cos_ref, sin_ref, k_ref, v_ref, kb_ref, vb_ref):
    x = x_ref[...]
    k = _rope(_dot(x, w_ref[:, :D_MODEL]), cos_ref[...], sin_ref[...])
    v = _dot(x, w_ref[:, D_MODEL:])
    k_ref[...] = k
    v_ref[...] = v
    kb_ref[...] = k.astype(BF16)
    vb_ref[...] = v.astype(BF16)


def _kv_proj(x, w, cos, sin, tm=512):
    m, k = x.shape
    tm = _row_tile(m, tm)
    row = lambda i: (i, 0)
    return pl.pallas_call(
        _kv_proj_kernel,
        grid=(m // tm,),
        in_specs=[pl.BlockSpec((tm, k), row),
                  _resident((k, 2 * D_MODEL)),
                  pl.BlockSpec((tm, LANES), row),
                  pl.BlockSpec((tm, LANES), row)],
        out_specs=[pl.BlockSpec((tm, D_MODEL), row)] * 4,
        out_shape=[jax.ShapeDtypeStruct((m, D_MODEL), F32)] * 2 + [jax.ShapeDtypeStruct((m, D_MODEL), BF16)] * 2,
        compiler_params=_cparams(("parallel",)),
        name="kv_proj",
    )(x, w, cos, sin)


def _conv_gate_kernel(x_ref, p_ref, ab_ref, cw_ref, alog_ref, dtb_ref, q_ref, k_ref, v_ref, gb_ref,
                      *, seg, t_real, chunk, blocks_per_seq):
    tm = x_ref.shape[0]
    row = lax.broadcasted_iota(jnp.int32, (tm, 1), 0)
    tloc = row % seg if seg < tm else row
    valid = tloc < t_real
    if seg == tm:
        not_first = (pl.program_id(0) % blocks_per_seq) != 0
        row8 = lax.broadcasted_iota(jnp.int32, (SUBLANES, 1), 0)

    for c in range(QKV_DIM // LANES):
        cs = slice(c * LANES, (c + 1) * LANES)
        xc = x_ref[:, cs]
        acc = cw_ref[CONV_W - 1:CONV_W, cs] * xc
        for j in range(1, CONV_W):
            sh = pltpu.roll(xc, j, 0)
            if seg == tm:
                halo = jnp.where(not_first, p_ref[:, cs], 0.0)
                head = jnp.where(row8 < j, pltpu.roll(halo, j, 0), sh[:SUBLANES])
                sh = jnp.concatenate([head, sh[SUBLANES:]], axis=0)
            else:
                prev = pltpu.roll(p_ref[:, cs], tm + j - SUBLANES, 0)
                sh = jnp.where(tloc >= j, sh, prev)
            acc = acc + cw_ref[CONV_W - 1 - j:CONV_W - j, cs] * sh
        y = _silu(acc)
        if c < 2 * N_HEADS:
            y = y * lax.rsqrt(jnp.sum(y * y, axis=-1, keepdims=True) + 1e-6)
        if c < N_HEADS:
            y = y * (HEAD_DIM ** -0.5)
        y = jnp.where(valid, y, 0.0)
        if c < N_HEADS:
            q_ref[:, cs] = y
        elif c < 2 * N_HEADS:
            k_ref[:, (c - N_HEADS) * LANES:(c - N_HEADS + 1) * LANES] = y
        else:
            v_ref[:, (c - 2 * N_HEADS) * LANES:(c - 2 * N_HEADS + 1) * LANES] = y

    ab = ab_ref[...]
    z = ab + dtb_ref[...]
    softplus = jnp.maximum(z, 0.0) + jnp.log1p(jnp.exp(-jnp.abs(z)))
    g = jnp.where(valid, -jnp.exp(alog_ref[...]) * softplus, 0.0)
    beta = jnp.where(valid, jax.nn.sigmoid(ab), 0.0)
    ri = lax.broadcasted_iota(jnp.int32, (tm, tm), 0)
    ci = lax.broadcasted_iota(jnp.int32, (tm, tm), 1)
    tril = jnp.where((ri // chunk == ci // chunk) & (ci <= ri), 1.0, 0.0)
    g_cum = _dot(tril, g, precision=HIGHEST)
    lane = lax.broadcasted_iota(jnp.int32, (tm, LANES), 1)
    gb_ref[...] = jnp.where(lane < N_HEADS, g_cum, beta)


def _conv_gate(qkv_pre, prev, ab, conv_w, a_log, dt_bias, *, seq_len, t_real, chunk, tm=256):
    m = qkv_pre.shape[0]
    if prev is None:
        tm = _row_tile(seq_len, tm)
        seg = tm
        blocks_per_seq = seq_len // tm
        tiles_per_block = tm // SUBLANES
        p_arr = qkv_pre
        p_spec = pl.BlockSpec((SUBLANES, QKV_DIM), lambda i: (jnp.maximum(i * tiles_per_block - 1, 0), 0))
    else:
        tm = _row_tile(m, tm)
        seg = seq_len
        blocks_per_seq = 1
        p_arr = prev
        p_spec = pl.BlockSpec((tm, QKV_DIM), lambda i: (i, 0))
    assert seg % chunk == 0 or chunk % seg == 0
    row = lambda i: (i, 0)
    const = lambda i: (0, 0)
    cw = jnp.zeros((SUBLANES, QKV_DIM), F32).at[:CONV_W].set(conv_w)
    alog = jnp.zeros((1, LANES), F32).at[0, :N_HEADS].set(a_log)
    dtb = jnp.zeros((1, LANES), F32).at[0, :N_HEADS].set(dt_bias)
    return pl.pallas_call(
        functools.partial(_conv_gate_kernel, seg=seg, t_real=t_real, chunk=min(chunk, seg),
                          blocks_per_seq=blocks_per_seq),
        grid=(m // tm,),
        in_specs=[pl.BlockSpec((tm, QKV_DIM), row), p_spec, pl.BlockSpec((tm, LANES), row),
                  pl.BlockSpec((SUBLANES, QKV_DIM), const), pl.BlockSpec((1, LANES), const),
                  pl.BlockSpec((1, LANES), const)],
        out_specs=[pl.BlockSpec((tm, D_MODEL), row)] * 3 + [pl.BlockSpec((tm, LANES), row)],
        out_shape=[jax.ShapeDtypeStruct((m, D_MODEL), F32)] * 3 + [jax.ShapeDtypeStruct((m, LANES), F32)],
        compiler_params=_cparams(("parallel",)),
        name="conv_gate",
    )(qkv_pre, p_arr, ab, cw, alog, dtb)


def _delta_prepare(qs, ks, vs, gs, bs):
    n = len(qs)
    c = qs[0].shape[0]
    ri = lax.broadcasted_iota(jnp.int32, (c, c), 0)
    ci = lax.broadcasted_iota(jnp.int32, (c, c), 1)
    causal = ci <= ri
    strict = ci < ri
    eye = ci == ri
    decay, kb, k_b, pw = [], [], [], []
    for i in range(n):
        g_row = jnp.sum(jnp.where(eye, gs[i], 0.0), axis=0, keepdims=True)
        decay.append(jnp.where(causal, jnp.exp(jnp.where(causal, gs[i] - g_row, 0.0)), 0.0))
        kb.append(ks[i] * bs[i])
        k_b.append(ks[i].astype(BF16))
    for i in range(n):
        pw.append(-jnp.where(strict, _dot_nt(kb[i].astype(BF16), k_b[i]) * decay[i], 0.0))
    inv_m1 = list(pw)
    for _ in range(int(math.log2(c)) - 1):
        for i in range(n):
            pw_b = pw[i].astype(BF16)
            pw[i] = _dot(pw_b, pw_b)
        for i in range(n):
            inv_m1[i] = inv_m1[i] + pw[i] + _dot(inv_m1[i].astype(BF16), pw[i].astype(BF16))
    out = []
    for i in range(n):
        e_g = jnp.exp(gs[i])
        rhs = jnp.concatenate([vs[i] * bs[i], kb[i] * e_g], axis=1)
        sol = rhs + _dot(inv_m1[i].astype(BF16), rhs.astype(BF16))
        attn = _dot_nt(qs[i].astype(BF16), k_b[i]) * decay[i]
        g_last = gs[i][c - 1:c, :]
        out.append(dict(u=sol[:, :HEAD_DIM], w=sol[:, HEAD_DIM:].astype(BF16), attn=attn.astype(BF16),
                        q_dec=(qs[i] * e_g).astype(BF16),
                        k_dec=(ks[i] * jnp.exp(g_last - gs[i])).astype(BF16), decay_last=jnp.exp(g_last)))
    return out


def _delta_apply(prep, states):
    n = len(prep)
    sb = [s.astype(BF16) for s in states]
    v_new = [(prep[i]["u"] - _dot(prep[i]["w"], sb[i])).astype(BF16) for i in range(n)]
    outs = [_dot(prep[i]["q_dec"], sb[i]) + _dot(prep[i]["attn"], v_new[i]) for i in range(n)]
    new_states = [states[i] * prep[i]["decay_last"] + _dot_tn(prep[i]["k_dec"], v_new[i]) for i in range(n)]
    return outs, new_states


def _delta_kernel(*refs, chunk, has_s0, has_buf):
    q_ref, k_ref, v_ref, gb_ref = refs[:4]
    s0_ref = refs[4] if has_s0 else None
    o_ref, s_ref = refs[4 + has_s0 + has_buf:]
    rows = q_ref.shape[0]

    @pl.when(pl.program_id(1) == 0)
    def _init():
        if has_s0:
            s_ref[...] = s0_ref[...]
        else:
            s_ref[...] = jnp.zeros_like(s_ref)

    n_chunks = rows // chunk
    qs, ks, vs, gs, bs = [], [], [], [], []
    for ci in range(n_chunks):
        rs = slice(ci * chunk, (ci + 1) * chunk)
        gb = gb_ref[rs, :]
        for h in range(N_HEADS):
            hs = slice(h * HEAD_DIM, (h + 1) * HEAD_DIM)
            qs.append(q_ref[rs, hs])
            ks.append(k_ref[rs, hs])
            vs.append(v_ref[rs, hs])
            gs.append(gb[:, h:h + 1])
            bs.append(gb[:, N_HEADS + h:N_HEADS + h + 1])
    prep = _delta_prepare(qs, ks, vs, gs, bs)
    n_states = s_ref.shape[0]
    if n_states == 1:
        states = [s_ref[0, h] for h in range(N_HEADS)]
        outs = []
        for ci in range(n_chunks):
            out_c, states = _delta_apply(prep[ci * N_HEADS:(ci + 1) * N_HEADS], states)
            outs += out_c
    else:
        states = [s_ref[ci, h] for ci in range(n_chunks) for h in range(N_HEADS)]
        outs, states = _delta_apply(prep, states)
    for ci in range(n_chunks):
        for h in range(N_HEADS):
            o_ref[ci * chunk:(ci + 1) * chunk, h * HEAD_DIM:(h + 1) * HEAD_DIM] = outs[ci * N_HEADS + h]
    for si in range(n_states):
        for h in range(N_HEADS):
            s_ref[si, h] = states[si * N_HEADS + h]


def _delta_rule(q, k, v, gb, s0_all, states_buf, layer, n_layers, *, n_seq, seq_len, chunk, rows_per_step=128,
                short_seqs_per_step=4):
    m = q.shape[0]
    chunk = min(chunk, seq_len)
    if seq_len == chunk:
        sps = _row_tile(n_seq, short_seqs_per_step)
        rows = sps * chunk
        grid = (n_seq // sps, 1)
        row = lambda s, i: (s, 0)
    else:
        sps = 1
        rows = _row_tile(seq_len, max(rows_per_step, chunk))
        steps = seq_len // rows
        grid = (n_seq, steps)
        row = lambda s, i: (s * steps + i, 0)
    state_spec = pl.BlockSpec((None, sps, N_HEADS, HEAD_DIM, HEAD_DIM), lambda s, i: (layer, s, 0, 0, 0))
    in_specs = [pl.BlockSpec((rows, D_MODEL), row)] * 3 + [pl.BlockSpec((rows, LANES), row)]
    args = [q, k, v, gb]
    if s0_all is not None:
        in_specs.append(state_spec)
        args.append(s0_all)
    aliases = {}
    if states_buf is not None:
        in_specs.append(pl.BlockSpec(memory_space=pl.ANY))
        args.append(states_buf)
        aliases = {len(args) - 1: 1}
    return pl.pallas_call(
        functools.partial(_delta_kernel, chunk=chunk, has_s0=s0_all is not None, has_buf=states_buf is not None),
        grid=grid,
        in_specs=in_specs,
        out_specs=[pl.BlockSpec((rows, D_MODEL), row), state_spec],
        out_shape=[jax.ShapeDtypeStruct((m, D_MODEL), F32),
                   jax.ShapeDtypeStruct((n_layers, n_seq, N_HEADS, HEAD_DIM, HEAD_DIM), F32)],
        input_output_aliases=aliases,
        compiler_params=_cparams(("parallel", "arbitrary")),
        name="delta_rule",
    )(*args)


def _mix_out_kernel(*refs, gated, post_scale):
    if gated:
        o_ref, z_ref, h_ref, gn_ref, w_ref, lg_ref, lb_ref, hf_ref, hb_ref = refs
    else:
        o_ref, h_ref, gn_ref, w_ref, lg_ref, lb_ref, hf_ref, hb_ref = refs
    parts = []
    for hd in range(N_HEADS):
        hs = slice(hd * HEAD_DIM, (hd + 1) * HEAD_DIM)
        o = o_ref[:, hs]
        y = o * lax.rsqrt(jnp.mean(o * o, axis=-1, keepdims=True) + RMS_EPS) * gn_ref[:, hs]
        if gated:
            y = y * _silu(z_ref[:, hs].astype(F32))
        else:
            y = y * post_scale
        parts.append(y.astype(BF16))
    mix = _dot(jnp.concatenate(parts, axis=1), w_ref[...])
    hn = _layer_norm(DEEPNORM_ALPHA * h_ref[...] + mix, lg_ref[...], lb_ref[...])
    hf_ref[...] = hn
    hb_ref[...] = hn.astype(BF16)


def _mix_out(o, z, h, g_norm, w, ln_g, ln_b, *, post_scale=1.0, tm=512):
    m = o.shape[0]
    tm = _row_tile(m, tm)
    row = lambda i: (i, 0)
    const = lambda i: (0, 0)
    gated = z is not None
    gn = jnp.tile(g_norm.reshape(1, HEAD_DIM), (1, N_HEADS))
    args = [o] + ([z] if gated else []) + [h, gn, w, ln_g.reshape(1, -1), ln_b.reshape(1, -1)]
    in_specs = ([pl.BlockSpec((tm, D_MODEL), row)] * (3 if gated else 2)
                + [pl.BlockSpec((1, D_MODEL), const), _resident((D_MODEL, D_MODEL)),
                   pl.BlockSpec((1, D_MODEL), const), pl.BlockSpec((1, D_MODEL), const)])
    return pl.pallas_call(
        functools.partial(_mix_out_kernel, gated=gated, post_scale=post_scale),
        grid=(m // tm,),
        in_specs=in_specs,
        out_specs=[pl.BlockSpec((tm, D_MODEL), row)] * 2,
        out_shape=[jax.ShapeDtypeStruct((m, D_MODEL), F32), jax.ShapeDtypeStruct((m, D_MODEL), BF16)],
        compiler_params=_cparams(("parallel",)),
        name="mix_out",
    )(*args)


def _dense_ffn_kernel(xb_ref, h_ref, wg_ref, wu_ref, wd_ref, lg_ref, lb_ref, hf_ref, hb_ref, *, tf):
    x = xb_ref[...]
    f_dim = wg_ref.shape[1]
    acc = DEEPNORM_ALPHA * h_ref[...]
    for c in range(0, f_dim, tf):
        cw = min(tf, f_dim - c)
        g = _dot(x, wg_ref[:, c:c + cw])
        u = _dot(x, wu_ref[:, c:c + cw])
        acc = acc + _dot((_silu(g) * u).astype(BF16), wd_ref[c:c + cw, :])
    hn = _layer_norm(acc, lg_ref[...], lb_ref[...])
    hf_ref[...] = hn
    hb_ref[...] = hn.astype(BF16)


def _dense_ffn(xb, h, wg, wu, wd, ln_g, ln_b, *, tm=512, tf=512):
    m = xb.shape[0]
    f_dim = wg.shape[1]
    tm = _row_tile(m, tm)
    row = lambda i: (i, 0)
    const = lambda i: (0, 0)
    return pl.pallas_call(
        functools.partial(_dense_ffn_kernel, tf=tf),
        grid=(m // tm,),
        in_specs=[pl.BlockSpec((tm, D_MODEL), row), pl.BlockSpec((tm, D_MODEL), row),
                  _resident((D_MODEL, f_dim)), _resident((D_MODEL, f_dim)), _resident((f_dim, D_MODEL)),
                  pl.BlockSpec((1, D_MODEL), const), pl.BlockSpec((1, D_MODEL), const)],
        out_specs=[pl.BlockSpec((tm, D_MODEL), row)] * 2,
        out_shape=[jax.ShapeDtypeStruct((m, D_MODEL), F32), jax.ShapeDtypeStruct((m, D_MODEL), BF16)],
        compiler_params=_cparams(("parallel",)),
        name="dense_ffn",
    )(xb, h, wg, wu, wd, ln_g.reshape(1, -1), ln_b.reshape(1, -1))


MOE_TM = 256
MOE_BM = 512
MOE_ALIGN = BF16_ROWS


def _router_kernel(h_ref, wr_ref, a_ref, r_ref, g_ref, cnt_ref):
    tm = h_ref.shape[0]
    logits = _dot_nt(wr_ref[...], h_ref[...], precision=HIGHEST)
    e_idx = lax.broadcasted_iota(jnp.int32, logits.shape, 0).astype(F32)
    m1 = jnp.max(logits, axis=0, keepdims=True)
    i1 = jnp.min(jnp.where(logits == m1, e_idx, float(N_EXPERTS)), axis=0, keepdims=True)
    first = e_idx == i1
    rest = jnp.where(first, -jnp.inf, logits)
    m2 = jnp.max(rest, axis=0, keepdims=True)
    i2 = jnp.min(jnp.where(rest == m2, e_idx, float(N_EXPERTS)), axis=0, keepdims=True)
    second = e_idx == i2
    ex = jnp.exp(m2 - m1)
    g1 = 1.0 / (1.0 + ex)
    g2 = ex / (1.0 + ex)
    routed = jnp.where(first | second, 1.0, 0.0)
    ji = lax.broadcasted_iota(jnp.int32, (tm, tm), 0)
    ii = lax.broadcasted_iota(jnp.int32, (tm, tm), 1)
    before = jnp.where(ji < ii, 1.0, 0.0).astype(BF16)
    a_ref[...] = routed
    r_ref[...] = _dot(routed.astype(BF16), before)
    g_ref[...] = jnp.where(first, g1, 0.0) + jnp.where(second, g2, 0.0)
    cnt_ref[0] = jnp.broadcast_to(jnp.sum(routed, axis=1, keepdims=True), (N_EXPERTS, LANES))


def _router(h, w_router_t, tm):
    m = h.shape[0]
    nb = m // tm
    col = lambda i: (0, i)
    return pl.pallas_call(
        _router_kernel,
        grid=(nb,),
        in_specs=[pl.BlockSpec((tm, D_MODEL), lambda i: (i, 0)),
                  pl.BlockSpec((N_EXPERTS, D_MODEL), lambda i: (0, 0))],
        out_specs=[pl.BlockSpec((N_EXPERTS, tm), col)] * 3
                  + [pl.BlockSpec((1, N_EXPERTS, LANES), lambda i: (i, 0, 0))],
        out_shape=[jax.ShapeDtypeStruct((N_EXPERTS, m), F32)] * 3
                  + [jax.ShapeDtypeStruct((nb, N_EXPERTS, LANES), F32)],
        compiler_params=_cparams(("parallel",)),
        name="moe_router",
    )(h, w_router_t)


def _dispatch_kernel(off_ref, xb_ref, a_ref, r_ref, rows_in_ref, rows_ref, slab_ref, sem_ref):
    del rows_in_ref
    b = pl.program_id(0)
    nb = pl.num_programs(0)
    slot = b % 2
    tm = xb_ref.shape[0]

    def slab_copy(which_slot, which_block, e):
        off = pl.multiple_of(off_ref[which_block * N_EXPERTS + e], MOE_ALIGN)
        return pltpu.make_async_copy(slab_ref.at[which_slot, e], rows_ref.at[pl.ds(off, tm)],
                                     sem_ref.at[which_slot, e])

    x = xb_ref[...]
    r_idx = lax.broadcasted_iota(jnp.int32, (tm, tm), 0).astype(F32)
    for e in range(N_EXPERTS):
        pick = jnp.where((r_idx == r_ref[e:e + 1, :]) & (a_ref[e:e + 1, :] > 0.0), 1.0, 0.0)
        slab_ref[slot, e] = _dot(pick.astype(BF16), x).astype(BF16)

    @pl.when(b > 0)
    def _previous_block_landed():
        for e in range(N_EXPERTS):
            slab_copy(1 - slot, b - 1, e).wait()

    for e in range(N_EXPERTS):
        slab_copy(slot, b, e).start()

    @pl.when(b == nb - 1)
    def _drain():
        for e in range(N_EXPERTS):
            slab_copy(slot, b, e).wait()


def _dispatch(xb, a_t, r_t, off, n_rows, tm):
    m = xb.shape[0]
    nb = m // tm
    rows_init = jnp.zeros((n_rows, D_MODEL), BF16)
    grid_spec = pltpu.PrefetchScalarGridSpec(
        num_scalar_prefetch=1,
        grid=(nb,),
        in_specs=[pl.BlockSpec((tm, D_MODEL), lambda b, off: (b, 0)),
                  pl.BlockSpec((N_EXPERTS, tm), lambda b, off: (0, b)),
                  pl.BlockSpec((N_EXPERTS, tm), lambda b, off: (0, b)),
                  pl.BlockSpec(memory_space=pl.ANY)],
        out_specs=pl.BlockSpec(memory_space=pl.ANY),
        scratch_shapes=[pltpu.VMEM((2, N_EXPERTS, tm, D_MODEL), BF16),
                        pltpu.SemaphoreType.DMA((2, N_EXPERTS))],
    )
    return pl.pallas_call(
        _dispatch_kernel,
        grid_spec=grid_spec,
        out_shape=jax.ShapeDtypeStruct((n_rows, D_MODEL), BF16),
        input_output_aliases={4: 0},
        compiler_params=_cparams(("arbitrary",)),
        name="moe_dispatch",
    )(off, xb, a_t, r_t, rows_init)


def _expert_ffn_kernel(be_ref, valid_ref, x_ref, wg_ref, wu_ref, wd_ref, y_ref, acc_ref):
    del be_ref
    g_idx = pl.program_id(0)
    f = pl.program_id(1)
    nf = pl.num_programs(1)
    is_valid = valid_ref[g_idx] > 0

    @pl.when(is_valid)
    def _compute():
        x = x_ref[...]
        g = _dot(x, wg_ref[0])
        u = _dot(x, wu_ref[0])
        part = _dot((_silu(g) * u).astype(BF16), wd_ref[0])

        @pl.when(f == 0)
        def _():
            acc_ref[...] = part

        @pl.when(f > 0)
        def _():
            acc_ref[...] += part

        @pl.when(f == nf - 1)
        def _():
            y_ref[...] = acc_ref[...].astype(y_ref.dtype)

    @pl.when(jnp.logical_not(is_valid) & (f == nf - 1))
    def _skip():
        y_ref[...] = jnp.zeros_like(y_ref)


def _expert_ffn(x_rows, blk_e, valid, wg, wu, wd, bm, tf):
    n_rows = x_rows.shape[0]
    f_dim = wg.shape[2]
    nf = f_dim // tf
    def f_blk(g, f, va):
        return jnp.where(va[g] > 0, f, nf - 1)

    grid_spec = pltpu.PrefetchScalarGridSpec(
        num_scalar_prefetch=2,
        grid=(n_rows // bm, nf),
        in_specs=[pl.BlockSpec((bm, D_MODEL), lambda g, f, be, va: (g, 0)),
                  pl.BlockSpec((1, D_MODEL, tf), lambda g, f, be, va: (be[g], 0, f_blk(g, f, va))),
                  pl.BlockSpec((1, D_MODEL, tf), lambda g, f, be, va: (be[g], 0, f_blk(g, f, va))),
                  pl.BlockSpec((1, tf, D_MODEL), lambda g, f, be, va: (be[g], f_blk(g, f, va), 0))],
        out_specs=pl.BlockSpec((bm, D_MODEL), lambda g, f, be, va: (g, 0)),
        scratch_shapes=[pltpu.VMEM((bm, D_MODEL), F32)],
    )
    return pl.pallas_call(
        _expert_ffn_kernel,
        grid_spec=grid_spec,
        out_shape=jax.ShapeDtypeStruct((n_rows, D_MODEL), BF16),
        compiler_params=_cparams(("parallel", "arbitrary")),
        name="moe_expert_ffn",
    )(blk_e, valid, x_rows, wg, wu, wd)


def _combine_kernel(off_ref, h_ref, a_ref, r_ref, g_ref, lg_ref, lb_ref, rows_ref, hf_ref, hb_ref,
                    slab_ref, sem_ref, acc_ref):
    b = pl.program_id(0)
    nb = pl.num_programs(0)
    slot = b % 2
    tm = h_ref.shape[0]

    def slab_copy(which_slot, which_block, e):
        off = pl.multiple_of(off_ref[which_block * N_EXPERTS + e], MOE_ALIGN)
        return pltpu.make_async_copy(rows_ref.at[pl.ds(off, tm)], slab_ref.at[which_slot, e],
                                     sem_ref.at[which_slot, e])

    @pl.when(b == 0)
    def _prime():
        for e in range(N_EXPERTS):
            slab_copy(0, 0, e).start()

    @pl.when(b + 1 < nb)
    def _prefetch():
        for e in range(N_EXPERTS):
            slab_copy(1 - slot, b + 1, e).start()

    acc_ref[...] = DEEPNORM_ALPHA * h_ref[...]
    r_idx = lax.broadcasted_iota(jnp.int32, (tm, tm), 1).astype(F32)
    for e in range(N_EXPERTS):
        pick = jnp.where((r_idx == r_ref[:, e:e + 1]) & (a_ref[:, e:e + 1] > 0.0), 1.0, 0.0)
        slab_copy(slot, b, e).wait()
        acc_ref[...] += g_ref[:, e:e + 1] * _dot(pick.astype(BF16), slab_ref[slot, e])
    hn = _layer_norm(acc_ref[...], lg_ref[...], lb_ref[...])
    hf_ref[...] = hn
    hb_ref[...] = hn.astype(BF16)


def _combine(h, a_c, r_c, g_c, ln_g, ln_b, y_rows, off, tm):
    m = h.shape[0]
    nb = m // tm
    blk = lambda b, off: (b, 0)
    const = lambda b, off: (0, 0)
    grid_spec = pltpu.PrefetchScalarGridSpec(
        num_scalar_prefetch=1,
        grid=(nb,),
        in_specs=[pl.BlockSpec((tm, D_MODEL), blk),
                  pl.BlockSpec((tm, N_EXPERTS), blk), pl.BlockSpec((tm, N_EXPERTS), blk),
                  pl.BlockSpec((tm, N_EXPERTS), blk),
                  pl.BlockSpec((1, D_MODEL), const), pl.BlockSpec((1, D_MODEL), const),
                  pl.BlockSpec(memory_space=pl.ANY)],
        out_specs=[pl.BlockSpec((tm, D_MODEL), blk)] * 2,
        scratch_shapes=[pltpu.VMEM((2, N_EXPERTS, tm, D_MODEL), BF16), pltpu.SemaphoreType.DMA((2, N_EXPERTS)),
                        pltpu.VMEM((tm, D_MODEL), F32)],
    )
    return pl.pallas_call(
        _combine_kernel,
        grid_spec=grid_spec,
        out_shape=[jax.ShapeDtypeStruct((m, D_MODEL), F32), jax.ShapeDtypeStruct((m, D_MODEL), BF16)],
        compiler_params=_cparams(("arbitrary",)),
        name="moe_combine",
    )(off, h, a_c, r_c, g_c, ln_g.reshape(1, -1), ln_b.reshape(1, -1), y_rows)


def _moe_ffn(h, hb, w_router, wg, wu, wd, ln_g, ln_b):
    m = h.shape[0]
    tm = _row_tile(m, MOE_TM)
    nb = m // tm
    bm = MOE_BM if m >= N_EXPERTS * MOE_BM else tm
    f_dim = wg.shape[2]
    tf = f_dim // 2 if (f_dim // 2) % 256 == 0 else f_dim
    a_t, r_t, g_t, cnts = _router(h, w_router.T, tm)

    cnt = cnts[:, :, 0].astype(jnp.int32)
    padded = (cnt + MOE_ALIGN - 1) // MOE_ALIGN * MOE_ALIGN
    base = jnp.cumsum(padded, axis=0) - padded
    used = jnp.sum(padded, axis=0)
    cap = (used + tm + bm - 1) // bm * bm
    region = jnp.cumsum(cap) - cap
    off = (region[None, :] + base).reshape(-1).astype(jnp.int32)
    max_rows = 2 * m + nb * N_EXPERTS * (MOE_ALIGN - 1) + N_EXPERTS * (tm + bm)
    n_rows = (max_rows + bm - 1) // bm * bm
    g_idx = jnp.arange(n_rows // bm, dtype=jnp.int32)
    region_blk = region // bm
    blk_e = (jnp.sum(g_idx[:, None] >= region_blk[None, :], axis=1) - 1).astype(jnp.int32)
    n_blk = (used + bm - 1) // bm
    valid = ((g_idx - region_blk[blk_e]) < n_blk[blk_e]).astype(jnp.int32)

    x_rows = _dispatch(hb, a_t, r_t, off, n_rows, tm)
    y_rows = _expert_ffn(x_rows, blk_e, valid, wg, wu, wd, bm, tf)
    return _combine(h, a_t.T, r_t.T, g_t.T, ln_g, ln_b, y_rows, off, tm)


def _lambda_value(lam_ref, lam_init):
    s1 = jnp.sum(lam_ref[0:1, :] * lam_ref[1:2, :], axis=1, keepdims=True)
    s2 = jnp.sum(lam_ref[2:3, :] * lam_ref[3:4, :], axis=1, keepdims=True)
    return jnp.exp(s1) - jnp.exp(s2) + lam_init


def _attn_prompt_kernel(qi_ref, kj_ref, lam_ref, q_ref, k_ref, v_ref, o_ref, qq_ref, m_ref, acc_ref,
                        *, lam_init):
    p = pl.program_id(2)
    i = qi_ref[p]
    j = kj_ref[p]
    tq = q_ref.shape[0]
    tk = k_ref.shape[0]

    @pl.when(j == 0)
    def _init():
        q = q_ref[...]
        lane = lax.broadcasted_iota(jnp.int32, q.shape, 1)
        zero = jnp.zeros_like(q)
        qq_ref[0:tq, :] = jnp.where(lane < D_HEAD_B, q, zero)
        qq_ref[tq:2 * tq, :] = jnp.where(lane >= D_HEAD_B, q, zero)
        m_ref[...] = jnp.full_like(m_ref, -jnp.inf)
        acc_ref[...] = jnp.zeros_like(acc_ref)

    def step(masked):
        s = _dot_nt(qq_ref[...], k_ref[...])
        if masked:
            qpos = lax.broadcasted_iota(jnp.int32, s.shape, 0) % tq
            kpos = lax.broadcasted_iota(jnp.int32, s.shape, 1)
            s = jnp.where(kpos <= qpos, s, -jnp.inf)
        m_prev = m_ref[...]
        m_new = jnp.maximum(m_prev, jnp.max(s, axis=1, keepdims=True))
        alpha = jnp.exp2(m_prev - m_new)
        pr = jnp.exp2(s - jnp.concatenate([m_new] * (tk // LANES), axis=1))
        acc_ref[:, HEAD_DIM:] = alpha * acc_ref[:, HEAD_DIM:] + jnp.sum(pr, axis=1, keepdims=True)
        acc_ref[:, :HEAD_DIM] = alpha * acc_ref[:, :HEAD_DIM] + _dot(pr.astype(BF16), v_ref[...])
        m_ref[...] = m_new

    @pl.when(j < i)
    def _full():
        step(False)

    @pl.when(j == i)
    def _diag():
        step(True)
        lam = _lambda_value(lam_ref, lam_init)
        o = acc_ref[:, :HEAD_DIM] / acc_ref[:, HEAD_DIM:]
        o_ref[...] = o[:tq] - lam * o[tq:]


def _attn_prompt(q, k, v, lam_p, lam_init, *, n_seq, seq_len, tq=512):
    m = q.shape[0]
    tq = _row_tile(seq_len, tq)
    nq = seq_len // tq
    pairs = [(i, j) for i in range(nq) for j in range(i + 1)]
    qi = jnp.asarray([p[0] for p in pairs], jnp.int32)
    kj = jnp.asarray([p[1] for p in pairs], jnp.int32)
    lam_pad = jnp.zeros((SUBLANES, LANES), F32).at[:4, :D_HEAD_B].set(lam_p)
    grid_spec = pltpu.PrefetchScalarGridSpec(
        num_scalar_prefetch=2,
        grid=(n_seq, N_HEADS, len(pairs)),
        in_specs=[pl.BlockSpec((SUBLANES, LANES), lambda b, h, p, qi, kj: (0, 0)),
                  pl.BlockSpec((tq, HEAD_DIM), lambda b, h, p, qi, kj: (b * nq + qi[p], h)),
                  pl.BlockSpec((tq, HEAD_DIM), lambda b, h, p, qi, kj: (b * nq + kj[p], h)),
                  pl.BlockSpec((tq, HEAD_DIM), lambda b, h, p, qi, kj: (b * nq + kj[p], h))],
        out_specs=pl.BlockSpec((tq, HEAD_DIM), lambda b, h, p, qi, kj: (b * nq + qi[p], h)),
        scratch_shapes=[pltpu.VMEM((2 * tq, HEAD_DIM), BF16), pltpu.VMEM((2 * tq, LANES), F32),
                        pltpu.VMEM((2 * tq, HEAD_DIM + LANES), F32)],
    )
    assert tq % LANES == 0
    return pl.pallas_call(
        functools.partial(_attn_prompt_kernel, lam_init=lam_init),
        grid_spec=grid_spec,
        out_shape=jax.ShapeDtypeStruct((m, D_MODEL), F32),
        compiler_params=_cparams(("parallel", "parallel", "arbitrary")),
        name="attn_prompt",
    )(qi, kj, lam_pad, q, k, v)


SAMPLE_PAGE_GROUPS = 2


def _attn_sample_kernel(*refs, pages_per_step, t_real, lam_init):
    pp = pages_per_step
    pt_ref, lam_ref, q_ref, kn_ref, vn_ref = refs[:5]
    k_refs = refs[5:5 + pp]
    v_refs = refs[5 + pp:5 + 2 * pp]
    o_ref, qbd_ref, spread_ref, m_ref, l_ref, acc_ref = refs[5 + 2 * pp:]
    del pt_ref
    j = pl.program_id(1)
    n_rows = qbd_ref.shape[0]
    page = k_refs[0].shape[1]
    rows_per_head = 2 * SAMPLE_PAD_T

    def online_update(s_list, pv_fn):
        m_prev = m_ref[...]
        m_cur = jnp.max(s_list[0], axis=1, keepdims=True)
        for s in s_list[1:]:
            m_cur = jnp.maximum(m_cur, jnp.max(s, axis=1, keepdims=True))
        m_new = jnp.maximum(m_prev, m_cur)
        alpha = jnp.exp2(m_prev - m_new)
        m_wide = jnp.concatenate([m_new] * (page // LANES), axis=1)
        prs = [jnp.exp2(s - m_wide) for s in s_list]
        l_add = jnp.sum(prs[0], axis=1, keepdims=True)
        for pr in prs[1:]:
            l_add = l_add + jnp.sum(pr, axis=1, keepdims=True)
        l_ref[...] = alpha * l_ref[...] + l_add
        acc_ref[...] = alpha * acc_ref[...] + pv_fn([pr.astype(BF16) for pr in prs])
        m_ref[...] = m_new

    @pl.when(j == 0)
    def _init():
        rep = jnp.concatenate([q_ref[...]] * (n_rows // SAMPLE_PAD_T), axis=0)
        r_grp = lax.broadcasted_iota(jnp.int32, rep.shape, 0) // SAMPLE_PAD_T
        l_grp = lax.broadcasted_iota(jnp.int32, rep.shape, 1) // D_HEAD_B
        qbd_ref[...] = jnp.where(r_grp == l_grp, rep, 0.0).astype(BF16)
        key = lax.broadcasted_iota(jnp.int32, spread_ref.shape, 0)
        col = lax.broadcasted_iota(jnp.int32, spread_ref.shape, 1)
        spread_ref[...] = jnp.where(col // N_HEADS == key, 1.0, 0.0).astype(BF16)
        m_ref[...] = jnp.full_like(m_ref, -jnp.inf)
        l_ref[...] = jnp.zeros_like(l_ref)
        acc_ref[...] = jnp.zeros_like(acc_ref)
        pad = jnp.zeros((page - SAMPLE_PAD_T, D_MODEL), F32)
        k_pad = jnp.concatenate([kn_ref[...], pad], axis=0).astype(BF16)
        v_pad = jnp.concatenate([vn_ref[...], pad], axis=0).astype(BF16)
        s = _dot_nt(qbd_ref[...], k_pad)
        t_q = lax.broadcasted_iota(jnp.int32, s.shape, 0) % SAMPLE_PAD_T
        t_k = lax.broadcasted_iota(jnp.int32, s.shape, 1)
        s = jnp.where((t_k <= t_q) & (t_k < t_real), s, -jnp.inf)

        def pv_new(prs):
            return jnp.concatenate(
                [_dot(prs[0][h * rows_per_head:(h + 1) * rows_per_head],
                      v_pad[:, h * HEAD_DIM:(h + 1) * HEAD_DIM]) for h in range(N_HEADS)], axis=0)

        online_update([s], pv_new)

    def softmax_stage(scores):
        m_g = jnp.max(scores[0], axis=1, keepdims=True)
        for s in scores[1:]:
            m_g = jnp.maximum(m_g, jnp.max(s, axis=1, keepdims=True))
        prs = [jnp.exp2(s - m_g) for s in scores]
        l_g = jnp.sum(prs[0], axis=1, keepdims=True)
        for pr in prs[1:]:
            l_g = l_g + jnp.sum(pr, axis=1, keepdims=True)
        return m_g, l_g, [pr.astype(BF16) for pr in prs]

    def spread_stage(prs):
        row_head = lax.broadcasted_iota(jnp.int32, (n_rows, page * N_HEADS), 0) // rows_per_head
        col_head = lax.broadcasted_iota(jnp.int32, (n_rows, page * N_HEADS), 1) % N_HEADS
        own_head = row_head == col_head
        return [jnp.where(own_head, _dot(pr, spread_ref[...]), 0.0).astype(BF16) for pr in prs]

    def pv_stage(pages, wides):
        total = None
        for p_i, wide in zip(pages, wides):
            part = _dot(wide, v_refs[p_i][0].astype(BF16))
            total = part if total is None else total + part
        return total

    def merge(streams):
        m_prev = m_ref[...]
        m_new = m_prev
        for m_g, _, _ in streams:
            m_new = jnp.maximum(m_new, m_g)
        scale = jnp.exp2(m_prev - m_new)
        l_new = scale * l_ref[...]
        acc_new = scale * acc_ref[...]
        for m_g, l_g, acc_g in streams:
            w = jnp.exp2(m_g - m_new)
            l_new = l_new + w * l_g
            acc_new = acc_new + w * acc_g
        m_ref[...] = m_new
        l_ref[...] = l_new
        acc_ref[...] = acc_new

    def page_scores(kb):
        rows_per_tile = n_rows // (D_MODEL // MXU_DEPTH)
        return jnp.concatenate(
            [_dot_nt(qbd_ref[t * rows_per_tile:(t + 1) * rows_per_tile, t * MXU_DEPTH:(t + 1) * MXU_DEPTH],
                     kb[:, t * MXU_DEPTH:(t + 1) * MXU_DEPTH]) for t in range(D_MODEL // MXU_DEPTH)], axis=0)

    n_groups = min(SAMPLE_PAGE_GROUPS, pp)
    groups = [list(range(g_i * pp // n_groups, (g_i + 1) * pp // n_groups)) for g_i in range(n_groups)]
    scores = [[page_scores(k_refs[p_i][0].astype(BF16)) for p_i in grp] for grp in groups]
    soft, spreads = [], []
    for g_i in range(len(groups)):
        soft.append(softmax_stage(scores[g_i]))
        spreads.append(spread_stage(soft[g_i][2]))
    merge([(soft[g_i][0], soft[g_i][1], pv_stage(groups[g_i], spreads[g_i])) for g_i in range(len(groups))])

    @pl.when(j == pl.num_programs(1) - 1)
    def _finish():
        lam = _lambda_value(lam_ref, lam_init)
        o = acc_ref[...] / l_ref[...]
        for h in range(N_HEADS):
            r1 = h * rows_per_head
            r2 = r1 + SAMPLE_PAD_T
            o_ref[:, h * HEAD_DIM:(h + 1) * HEAD_DIM] = o[r1:r1 + SAMPLE_PAD_T] - lam * o[r2:r2 + SAMPLE_PAD_T]


def _attn_sample(q, k_new, v_new, cache_k, cache_v, page_table, lam_p, lam_init, *, t_real, pages_per_step=8):
    m = q.shape[0]
    n_seq, n_pages = page_table.shape
    page = cache_k.shape[1]
    pp = pages_per_step
    while n_pages % pp:
        pp //= 2
    lam_pad = jnp.zeros((SUBLANES, LANES), F32).at[:4, :D_HEAD_B].set(lam_p)
    n_rows = 2 * N_HEADS * SAMPLE_PAD_T
    seq = lambda s, j, pt: (s, 0)

    def page_spec(p_i, rows, width):
        return pl.BlockSpec((1, rows, width), lambda s, j, pt: (pt[s * n_pages + j * pp + p_i], 0, 0))

    grid_spec = pltpu.PrefetchScalarGridSpec(
        num_scalar_prefetch=1,
        grid=(n_seq, n_pages // pp),
        in_specs=[pl.BlockSpec((SUBLANES, LANES), lambda s, j, pt: (0, 0)),
                  pl.BlockSpec((SAMPLE_PAD_T, D_MODEL), seq), pl.BlockSpec((SAMPLE_PAD_T, D_MODEL), seq),
                  pl.BlockSpec((SAMPLE_PAD_T, D_MODEL), seq)]
                 + [page_spec(p_i, page, D_MODEL) for p_i in range(pp)]
                 + [page_spec(p_i, page * N_HEADS, HEAD_DIM) for p_i in range(pp)],
        out_specs=pl.BlockSpec((SAMPLE_PAD_T, D_MODEL), seq),
        scratch_shapes=[pltpu.VMEM((n_rows, D_MODEL), BF16), pltpu.VMEM((page, page * N_HEADS), BF16),
                        pltpu.VMEM((n_rows, LANES), F32), pltpu.VMEM((n_rows, LANES), F32),
                        pltpu.VMEM((n_rows, HEAD_DIM), F32)],
    )
    assert page == LANES and n_rows == LANES
    return pl.pallas_call(
        functools.partial(_attn_sample_kernel, pages_per_step=pp, t_real=t_real, lam_init=lam_init),
        grid_spec=grid_spec,
        out_shape=jax.ShapeDtypeStruct((m, D_MODEL), F32),
        compiler_params=_cparams(("parallel", "arbitrary")),
        name="attn_sample",
    )(page_table.reshape(-1), lam_pad, q, k_new, v_new, *([cache_k] * pp), *([cache_v] * pp))


def _rope_tables(pos):
    half = D_HEAD_B // 2
    inv_freq = ROPE_THETA ** (-jnp.arange(half, dtype=F32) / half)
    ang = pos.astype(F32)[:, None] * inv_freq[None, :]
    cos, sin = jnp.cos(ang), jnp.sin(ang)
    cos128 = jnp.tile(cos, (1, LANES // half))
    sin128 = jnp.tile(jnp.concatenate([-sin, sin], axis=1), (1, LANES // D_HEAD_B))
    return cos128, sin128


def _trunk(x, pos, conv_state, delta_state, past, wts, *, n_seq, seq_len, t_real):
    h = x
    hb = x.astype(BF16)
    cos128, sin128 = _rope_tables(pos)
    new_convs = []
    new_states = jnp.zeros((DEPTH // 2, n_seq, N_HEADS, HEAD_DIM, HEAD_DIM), F32)
    kf = vf = kb = vb = None
    for l in range(DEPTH):
        if l < DEPTH // 2:
            qkv_pre, z, ab = _linear(hb, wts["w_in"][l], (QKV_DIM, D_MODEL, LANES), (F32, BF16, F32))
            pre3 = qkv_pre.reshape(n_seq, seq_len, QKV_DIM)
            new_convs.append(pre3[:, t_real - (CONV_W - 1):t_real])
            if conv_state is None:
                prev = None
            else:
                prev = jnp.pad(conv_state[l], ((0, 0), (SAMPLE_PAD_T - (CONV_W - 1), 0), (0, 0)))
                prev = prev.reshape(n_seq * SAMPLE_PAD_T, QKV_DIM)
            q, k, v, gb = _conv_gate(qkv_pre, prev, ab, wts["w_conv"][l], wts["a_log"][l], wts["dt_bias"][l],
                                     seq_len=seq_len, t_real=t_real, chunk=DELTA_CHUNK)
            o, new_states = _delta_rule(q, k, v, gb, delta_state, new_states, l, DEPTH // 2,
                                        n_seq=n_seq, seq_len=seq_len, chunk=DELTA_CHUNK)
            h, hb = _mix_out(o, z, h, wts["g_norm"][l], wts["w_out"][l], wts["ln_mix_g"][l], wts["ln_mix_b"][l])
        else:
            j = l - DEPTH // 2
            lam_init = 0.8 - 0.6 * math.exp(-0.3 * l)
            if past is None:
                q = _q_proj(hb, wts["w_q"][j], cos128, sin128, BF16)
                o = _attn_prompt(q, kb, vb, wts["lambda"][j], lam_init, n_seq=n_seq, seq_len=seq_len)
            else:
                q = _q_proj(hb, wts["w_q"][j], cos128, sin128, F32)
                o = _attn_sample(q, kf, vf, past[0], past[1], past[2], wts["lambda"][j], lam_init, t_real=t_real)
            h, hb = _mix_out(o, None, h, wts["g_sub"][j], wts["w_o"][j], wts["ln_mix_g"][l], wts["ln_mix_b"][l],
                             post_scale=1.0 - lam_init)
        if l % 2 == 0:
            i = l // 2
            h, hb = _dense_ffn(hb, h, wts["w_gate_d"][i], wts["w_up_d"][i], wts["w_down_d"][i],
                               wts["ln_ffn_g"][l], wts["ln_ffn_b"][l])
        else:
            i = l // 2
            h, hb = _moe_ffn(h, hb, wts["w_router"][i], wts["w_gate_e"][i], wts["w_up_e"][i], wts["w_down_e"][i],
                             wts["ln_ffn_g"][l], wts["ln_ffn_b"][l])
        if l == DEPTH // 2 - 1:
            kf, vf, kb, vb = _kv_proj(hb, wts["w_kv"], cos128, sin128)
    return h, jnp.stack(new_convs), new_states, kf, vf


def kernel(x_prompt, x_sample, state_delta, state_conv, cache_k, cache_v, page_table, w_in_a, w_conv_a, a_log_a, dt_bias_a, g_norm_a, w_out_a, w_kv, w_q_b, lambda_b, g_sub_b, w_o_b, ln_mix_g, ln_mix_b, ln_ffn_g, ln_ffn_b, w_gate_d, w_up_d, w_down_d, w_router, w_gate_e, w_up_e, w_down_e):
    n_a = w_in_a.shape[0]
    bp, tp, _ = x_prompt.shape
    bs, ts, _ = x_sample.shape
    assert CONV_W - 1 <= ts <= SAMPLE_PAD_T and tp % SUBLANES == 0

    w_in = jnp.pad(w_in_a, ((0, 0), (0, 0), (0, LANES - 2 * N_HEADS))).astype(BF16)
    wts = dict(
        w_in=w_in, w_conv=w_conv_a, a_log=a_log_a, dt_bias=dt_bias_a, g_norm=g_norm_a,
        w_out=w_out_a.astype(BF16), w_kv=w_kv.astype(BF16), w_q=w_q_b.astype(BF16), g_sub=g_sub_b,
        w_o=w_o_b.astype(BF16), ln_mix_g=ln_mix_g, ln_mix_b=ln_mix_b, ln_ffn_g=ln_ffn_g, ln_ffn_b=ln_ffn_b,
        w_gate_d=w_gate_d.astype(BF16), w_up_d=w_up_d.astype(BF16), w_down_d=w_down_d.astype(BF16),
        w_router=w_router, w_gate_e=w_gate_e.astype(BF16), w_up_e=w_up_e.astype(BF16),
        w_down_e=w_down_e.astype(BF16))
    wts["lambda"] = lambda_b

    pos_p = jnp.tile(jnp.arange(tp), bp)
    y_p, conv_p, delta_p, k_p, v_p = _trunk(x_prompt.reshape(bp * tp, D_MODEL), pos_p, None, None, None, wts,
                                            n_seq=bp, seq_len=tp, t_real=tp)

    n_pages = page_table.shape[1]
    page = cache_k.shape[1]
    past_len = n_pages * page
    xs = jnp.pad(x_sample, ((0, 0), (0, SAMPLE_PAD_T - ts), (0, 0))).reshape(bs * SAMPLE_PAD_T, D_MODEL)
    pos_s = jnp.tile(past_len + jnp.arange(SAMPLE_PAD_T), bs)
    past = (cache_k.reshape(cache_k.shape[0], page, D_MODEL),
            cache_v.reshape(cache_v.shape[0], page * N_HEADS, HEAD_DIM), page_table)
    y_s, conv_s, delta_s, k_s, v_s = _trunk(xs, pos_s, state_conv, state_delta, past, wts,
                                            n_seq=bs, seq_len=SAMPLE_PAD_T, t_real=ts)

    def unpad(a):
        return a.reshape(bs, SAMPLE_PAD_T, D_MODEL)[:, :ts]

    return (y_p.reshape(bp, tp, D_MODEL), unpad(y_s), delta_p, conv_p,
            k_p.reshape(bp, tp, N_HEADS, 2, D_HEAD_B), v_p.reshape(bp, tp, N_HEADS, HEAD_DIM),
            delta_s, conv_s,
            unpad(k_s).reshape(bs, ts, N_HEADS, 2, D_HEAD_B), unpad(v_s).reshape(bs, ts, N_HEADS, HEAD_DIM))
```

```python
import functools
import math

import jax
import jax.numpy as jnp
from jax import lax
from jax.experimental import pallas as pl
from jax.experimental.pallas import tpu as pltpu

F32 = jnp.float32
BF16 = jnp.bfloat16
HIGHEST = lax.Precision.HIGHEST

LANES = 128
SUBLANES = 8
BF16_ROWS = 16
MXU_DEPTH = 256
VMEM_LIMIT = 56 * 1024 * 1024

D_MODEL = 1024
N_HEADS = 8
HEAD_DIM = 128
QKV_DIM = 3 * D_MODEL
CONV_W = 4
DELTA_CHUNK = 64
D_HEAD_B = 64
ROPE_THETA = 10000.0
N_EXPERTS = 8
DEPTH = 4
DEEPNORM_ALPHA = (2.0 * DEPTH) ** 0.25
LN_EPS = 1e-5
RMS_EPS = 1e-6
SAMPLE_PAD_T = 8


def _cparams(semantics):
    return pltpu.CompilerParams(dimension_semantics=semantics, vmem_limit_bytes=VMEM_LIMIT)


def _dot(a, b, precision=None):
    return jnp.dot(a, b, preferred_element_type=F32, precision=precision)


def _dot_nt(a, b, precision=None):
    return lax.dot_general(a, b, (((1,), (1,)), ((), ())), preferred_element_type=F32, precision=precision)


def _dot_tn(a, b, precision=None):
    return lax.dot_general(a, b, (((0,), (0,)), ((), ())), preferred_element_type=F32, precision=precision)


def _layer_norm(y, g, b):
    mu = jnp.mean(y, axis=-1, keepdims=True)
    d = y - mu
    var = jnp.mean(d * d, axis=-1, keepdims=True)
    return d * lax.rsqrt(var + LN_EPS) * g + b


def _silu(x):
    return x * jax.nn.sigmoid(x)


def _resident(shape):
    return pl.BlockSpec(shape, lambda i: (0,) * len(shape), pipeline_mode=pl.Buffered(1))


def _row_tile(m, pref):
    t = min(pref, m)
    while m % t:
        t //= 2
    return t


def _linear_kernel(x_ref, w_ref, *out_refs, widths, tn):
    x = x_ref[...]
    col = 0
    for o_ref, width in zip(out_refs, widths):
        for c in range(0, width, tn):
            cw = min(tn, width - c)
            o_ref[:, c:c + cw] = _dot(x, w_ref[:, col + c:col + c + cw]).astype(o_ref.dtype)
        col += width


def _linear(x, w, widths, dtypes, tm=512, tn=512):
    m, k = x.shape
    tm = _row_tile(m, tm)
    n = sum(widths)
    return pl.pallas_call(
        functools.partial(_linear_kernel, widths=tuple(widths), tn=tn),
        grid=(m // tm,),
        in_specs=[pl.BlockSpec((tm, k), lambda i: (i, 0)),
                  _resident((k, n))],
        out_specs=[pl.BlockSpec((tm, wd), lambda i: (i, 0)) for wd in widths],
        out_shape=[jax.ShapeDtypeStruct((m, wd), dt) for wd, dt in zip(widths, dtypes)],
        compiler_params=_cparams(("parallel",)),
        name="linear",
    )(x, w)


def _rope(x, cos128, sin128):
    n = x.shape[1]
    reps = n // LANES
    cos_t = jnp.concatenate([cos128] * reps, axis=1)
    sin_t = jnp.concatenate([sin128] * reps, axis=1)
    lane = lax.broadcasted_iota(jnp.int32, x.shape, 1)
    first_half = (lane % D_HEAD_B) < (D_HEAD_B // 2)
    rot = jnp.where(first_half, pltpu.roll(x, n - D_HEAD_B // 2, 1), pltpu.roll(x, D_HEAD_B // 2, 1))
    return x * cos_t + rot * sin_t


def _q_proj_kernel(x_ref, w_ref, cos_ref, sin_ref, q_ref, *, scale):
    q = _dot(x_ref[...], w_ref[...])
    q_ref[...] = (_rope(q, cos_ref[...], sin_ref[...]) * scale).astype(q_ref.dtype)


def _q_proj(x, w, cos, sin, out_dtype, tm=512):
    m, k = x.shape
    tm = _row_tile(m, tm)
    return pl.pallas_call(
        functools.partial(_q_proj_kernel, scale=D_HEAD_B ** -0.5 * math.log2(math.e)),
        grid=(m // tm,),
        in_specs=[pl.BlockSpec((tm, k), lambda i: (i, 0)),
                  _resident((k, D_MODEL)),
                  pl.BlockSpec((tm, LANES), lambda i: (i, 0)),
                  pl.BlockSpec((tm, LANES), lambda i: (i, 0))],
        out_specs=pl.BlockSpec((tm, D_MODEL), lambda i: (i, 0)),
        out_shape=jax.ShapeDtypeStruct((m, D_MODEL), out_dtype),
        compiler_params=_cparams(("parallel",)),
        name="q_proj",
    )(x, w, cos, sin)


def _kv_proj_kernel(x_ref, w_ref, cos_ref, sin_ref, k_ref, v_ref, kb_ref, vb_ref):
    x = x_ref[...]
    k = _rope(_dot(x, w_ref[:, :D_MODEL]), cos_ref[...], sin_ref[...])
    v = _dot(x, w_ref[:, D_MODEL:])
    k_ref[...] = k
    v_ref[...] = v
    kb_ref[...] = k.astype(BF16)
    vb_ref[...] = v.astype(BF16)


def _kv_proj(x, w, cos, sin, tm=512):
    m, k = x.shape
    tm = _row_tile(m, tm)
    row = lambda i: (i, 0)
    return pl.pallas_call(
        _kv_proj_kernel,
        grid=(m // tm,),
        in_specs=[pl.BlockSpec((tm, k), row),
                  _resident((k, 2 * D_MODEL)),
                  pl.BlockSpec((tm, LANES), row),
                  pl.BlockSpec((tm, LANES), row)],
        out_specs=[pl.BlockSpec((tm, D_MODEL), row)] * 4,
        out_shape=[jax.ShapeDtypeStruct((m, D_MODEL), F32)] * 2 + [jax.ShapeDtypeStruct((m, D_MODEL), BF16)] * 2,
        compiler_params=_cparams(("parallel",)),
        name="kv_proj",
    )(x, w, cos, sin)


def _conv_gate_kernel(x_ref, p_ref, ab_ref, cw_ref, alog_ref, dtb_ref, q_ref, k_ref, v_ref, gb_ref,
                      *, seg, t_real, chunk, blocks_per_seq):
    tm = x_ref.shape[0]
    row = lax.broadcasted_iota(jnp.int32, (tm, 1), 0)
    tloc = row % seg if seg < tm else row
    valid = tloc < t_real
    if seg == tm:
        not_first = (pl.program_id(0) % blocks_per_seq) != 0
        row8 = lax.broadcasted_iota(jnp.int32, (SUBLANES, 1), 0)

    for c in range(QKV_DIM // LANES):
        cs = slice(c * LANES, (c + 1) * LANES)
        xc = x_ref[:, cs]
        acc = cw_ref[CONV_W - 1:CONV_W, cs] * xc
        for j in range(1, CONV_W):
            sh = pltpu.roll(xc, j, 0)
            if seg == tm:
                halo = jnp.where(not_first, p_ref[:, cs], 0.0)
                head = jnp.where(row8 < j, pltpu.roll(halo, j, 0), sh[:SUBLANES])
                sh = jnp.concatenate([head, sh[SUBLANES:]], axis=0)
            else:
                prev = pltpu.roll(p_ref[:, cs], tm + j - SUBLANES, 0)
                sh = jnp.where(tloc >= j, sh, prev)
            acc = acc + cw_ref[CONV_W - 1 - j:CONV_W - j, cs] * sh
        y = _silu(acc)
        if c < 2 * N_HEADS:
            y = y * lax.rsqrt(jnp.sum(y * y, axis=-1, keepdims=True) + 1e-6)
        if c < N_HEADS:
            y = y * (HEAD_DIM ** -0.5)
        y = jnp.where(valid, y, 0.0)
        if c < N_HEADS:
            q_ref[:, cs] = y
        elif c < 2 * N_HEADS:
            k_ref[:, (c - N_HEADS) * LANES:(c - N_HEADS + 1) * LANES] = y
        else:
            v_ref[:, (c - 2 * N_HEADS) * LANES:(c - 2 * N_HEADS + 1) * LANES] = y

    ab = ab_ref[...]
    z = ab + dtb_ref[...]
    softplus = jnp.maximum(z, 0.0) + jnp.log1p(jnp.exp(-jnp.abs(z)))
    g = jnp.where(valid, -jnp.exp(alog_ref[...]) * softplus, 0.0)
    beta = jnp.where(valid, jax.nn.sigmoid(ab), 0.0)
    ri = lax.broadcasted_iota(jnp.int32, (tm, tm), 0)
    ci = lax.broadcasted_iota(jnp.int32, (tm, tm), 1)
    tril = jnp.where((ri // chunk == ci // chunk) & (ci <= ri), 1.0, 0.0)
    g_cum = _dot(tril, g, precision=HIGHEST)
    lane = lax.broadcasted_iota(jnp.int32, (tm, LANES), 1)
    gb_ref[...] = jnp.where(lane < N_HEADS, g_cum, beta)


def _conv_gate(qkv_pre, prev, ab, conv_w, a_log, dt_bias, *, seq_len, t_real, chunk, tm=256):
    m = qkv_pre.shape[0]
    if prev is None:
        tm = _row_tile(seq_len, tm)
        seg = tm
        blocks_per_seq = seq_len // tm
        tiles_per_block = tm // SUBLANES
        p_arr = qkv_pre
        p_spec = pl.BlockSpec((SUBLANES, QKV_DIM), lambda i: (jnp.maximum(i * tiles_per_block - 1, 0), 0))
    else:
        tm = _row_tile(m, tm)
        seg = seq_len
        blocks_per_seq = 1
        p_arr = prev
        p_spec = pl.BlockSpec((tm, QKV_DIM), lambda i: (i, 0))
    assert seg % chunk == 0 or chunk % seg == 0
    row = lambda i: (i, 0)
    const = lambda i: (0, 0)
    cw = jnp.zeros((SUBLANES, QKV_DIM), F32).at[:CONV_W].set(conv_w)
    alog = jnp.zeros((1, LANES), F32).at[0, :N_HEADS].set(a_log)
    dtb = jnp.zeros((1, LANES), F32).at[0, :N_HEADS].set(dt_bias)
    return pl.pallas_call(
        functools.partial(_conv_gate_kernel, seg=seg, t_real=t_real, chunk=min(chunk, seg),
                          blocks_per_seq=blocks_per_seq),
        grid=(m // tm,),
        in_specs=[pl.BlockSpec((tm, QKV_DIM), row), p_spec, pl.BlockSpec((tm, LANES), row),
                  pl.BlockSpec((SUBLANES, QKV_DIM), const), pl.BlockSpec((1, LANES), const),
                  pl.BlockSpec((1, LANES), const)],
        out_specs=[pl.BlockSpec((tm, D_MODEL), row)] * 3 + [pl.BlockSpec((tm, LANES), row)],
        out_shape=[jax.ShapeDtypeStruct((m, D_MODEL), F32)] * 3 + [jax.ShapeDtypeStruct((m, LANES), F32)],
        compiler_params=_cparams(("parallel",)),
        name="conv_gate",
    )(qkv_pre, p_arr, ab, cw, alog, dtb)


def _delta_prepare(qs, ks, vs, gs, bs):
    n = len(qs)
    c = qs[0].shape[0]
    ri = lax.broadcasted_iota(jnp.int32, (c, c), 0)
    ci = lax.broadcasted_iota(jnp.int32, (c, c), 1)
    causal = ci <= ri
    strict = ci < ri
    eye = ci == ri
    decay, kb, k_b, pw = [], [], [], []
    for i in range(n):
        g_row = jnp.sum(jnp.where(eye, gs[i], 0.0), axis=0, keepdims=True)
        decay.append(jnp.where(causal, jnp.exp(jnp.where(causal, gs[i] - g_row, 0.0)), 0.0))
        kb.append(ks[i] * bs[i])
        k_b.append(ks[i].astype(BF16))
    for i in range(n):
        pw.append(-jnp.where(strict, _dot_nt(kb[i].astype(BF16), k_b[i]) * decay[i], 0.0))
    inv_m1 = list(pw)
    for _ in range(int(math.log2(c)) - 1):
        for i in range(n):
            pw_b = pw[i].astype(BF16)
            pw[i] = _dot(pw_b, pw_b)
        for i in range(n):
            inv_m1[i] = inv_m1[i] + pw[i] + _dot(inv_m1[i].astype(BF16), pw[i].astype(BF16))
    out = []
    for i in range(n):
        e_g = jnp.exp(gs[i])
        rhs = jnp.concatenate([vs[i] * bs[i], kb[i] * e_g], axis=1)
        sol = rhs + _dot(inv_m1[i].astype(BF16), rhs.astype(BF16))
        attn = _dot_nt(qs[i].astype(BF16), k_b[i]) * decay[i]
        g_last = gs[i][c - 1:c, :]
        out.append(dict(u=sol[:, :HEAD_DIM], w=sol[:, HEAD_DIM:].astype(BF16), attn=attn.astype(BF16),
                        q_dec=(qs[i] * e_g).astype(BF16),
                        k_dec=(ks[i] * jnp.exp(g_last - gs[i])).astype(BF16), decay_last=jnp.exp(g_last)))
    return out


def _delta_apply(prep, states):
    n = len(prep)
    sb = [s.astype(BF16) for s in states]
    v_new = [(prep[i]["u"] - _dot(prep[i]["w"], sb[i])).astype(BF16) for i in range(n)]
    outs = [_dot(prep[i]["q_dec"], sb[i]) + _dot(prep[i]["attn"], v_new[i]) for i in range(n)]
    new_states = [states[i] * prep[i]["decay_last"] + _dot_tn(prep[i]["k_dec"], v_new[i]) for i in range(n)]
    return outs, new_states


def _delta_kernel(*refs, chunk, has_s0, has_buf):
    q_ref, k_ref, v_ref, gb_ref = refs[:4]
    s0_ref = refs[4] if has_s0 else None
    o_ref, s_ref = refs[4 + has_s0 + has_buf:]
    rows = q_ref.shape[0]

    @pl.when(pl.program_id(1) == 0)
    def _init():
        if has_s0:
            s_ref[...] = s0_ref[...]
        else:
            s_ref[...] = jnp.zeros_like(s_ref)

    n_chunks = rows // chunk
    qs, ks, vs, gs, bs = [], [], [], [], []
    for ci in range(n_chunks):
        rs = slice(ci * chunk, (ci + 1) * chunk)
        gb = gb_ref[rs, :]
        for h in range(N_HEADS):
            hs = slice(h * HEAD_DIM, (h + 1) * HEAD_DIM)
            qs.append(q_ref[rs, hs])
            ks.append(k_ref[rs, hs])
            vs.append(v_ref[rs, hs])
            gs.append(gb[:, h:h + 1])
            bs.append(gb[:, N_HEADS + h:N_HEADS + h + 1])
    prep = _delta_prepare(qs, ks, vs, gs, bs)
    n_states = s_ref.shape[0]
    if n_states == 1:
        states = [s_ref[0, h] for h in range(N_HEADS)]
        outs = []
        for ci in range(n_chunks):
            out_c, states = _delta_apply(prep[ci * N_HEADS:(ci + 1) * N_HEADS], states)
            outs += out_c
    else:
        states = [s_ref[ci, h] for ci in range(n_chunks) for h in range(N_HEADS)]
        outs, states = _delta_apply(prep, states)
    for ci in range(n_chunks):
        for h in range(N_HEADS):
            o_ref[ci * chunk:(ci + 1) * chunk, h * HEAD_DIM:(h + 1) * HEAD_DIM] = outs[ci * N_HEADS + h]
    for si in range(n_states):
        for h in range(N_HEADS):
            s_ref[si, h] = states[si * N_HEADS + h]


def _delta_rule(q, k, v, gb, s0_all, states_buf, layer, n_layers, *, n_seq, seq_len, chunk, rows_per_step=256,
                short_seqs_per_step=4):
    m = q.shape[0]
    chunk = min(chunk, seq_len)
    if seq_len == chunk:
        sps = _row_tile(n_seq, short_seqs_per_step)
        rows = sps * chunk
        grid = (n_seq // sps, 1)
        row = lambda s, i: (s, 0)
    else:
        sps = 1
        rows = _row_tile(seq_len, max(rows_per_step, chunk))
        steps = seq_len // rows
        grid = (n_seq, steps)
        row = lambda s, i: (s * steps + i, 0)
    state_spec = pl.BlockSpec((None, sps, N_HEADS, HEAD_DIM, HEAD_DIM), lambda s, i: (layer, s, 0, 0, 0))
    in_specs = [pl.BlockSpec((rows, D_MODEL), row)] * 3 + [pl.BlockSpec((rows, LANES), row)]
    args = [q, k, v, gb]
    if s0_all is not None:
        in_specs.append(state_spec)
        args.append(s0_all)
    aliases = {}
    if states_buf is not None:
        in_specs.append(pl.BlockSpec(memory_space=pl.ANY))
        args.append(states_buf)
        aliases = {len(args) - 1: 1}
    return pl.pallas_call(
        functools.partial(_delta_kernel, chunk=chunk, has_s0=s0_all is not None, has_buf=states_buf is not None),
        grid=grid,
        in_specs=in_specs,
        out_specs=[pl.BlockSpec((rows, D_MODEL), row), state_spec],
        out_shape=[jax.ShapeDtypeStruct((m, D_MODEL), F32),
                   jax.ShapeDtypeStruct((n_layers, n_seq, N_HEADS, HEAD_DIM, HEAD_DIM), F32)],
        input_output_aliases=aliases,
        compiler_params=_cparams(("parallel", "arbitrary")),
        name="delta_rule",
    )(*args)


def _mix_out_kernel(*refs, gated, post_scale):
    if gated:
        o_ref, z_ref, h_ref, gn_ref, w_ref, lg_ref, lb_ref, hf_ref, hb_ref = refs
    else:
        o_ref, h_ref, gn_ref, w_ref, lg_ref, lb_ref, hf_ref, hb_ref = refs
    parts = []
    for hd in range(N_HEADS):
        hs = slice(hd * HEAD_DIM, (hd + 1) * HEAD_DIM)
        o = o_ref[:, hs]
        y = o * lax.rsqrt(jnp.mean(o * o, axis=-1, keepdims=True) + RMS_EPS) * gn_ref[:, hs]
        if gated:
            y = y * _silu(z_ref[:, hs].astype(F32))
        else:
            y = y * post_scale
        parts.append(y.astype(BF16))
    mix = _dot(jnp.concatenate(parts, axis=1), w_ref[...])
    hn = _layer_norm(DEEPNORM_ALPHA * h_ref[...] + mix, lg_ref[...], lb_ref[...])
    hf_ref[...] = hn
    hb_ref[...] = hn.astype(BF16)


def _mix_out(o, z, h, g_norm, w, ln_g, ln_b, *, post_scale=1.0, tm=512):
    m = o.shape[0]
    tm = _row_tile(m, tm)
    row = lambda i: (i, 0)
    const = lambda i: (0, 0)
    gated = z is not None
    gn = jnp.tile(g_norm.reshape(1, HEAD_DIM), (1, N_HEADS))
    args = [o] + ([z] if gated else []) + [h, gn, w, ln_g.reshape(1, -1), ln_b.reshape(1, -1)]
    in_specs = ([pl.BlockSpec((tm, D_MODEL), row)] * (3 if gated else 2)
                + [pl.BlockSpec((1, D_MODEL), const), _resident((D_MODEL, D_MODEL)),
                   pl.BlockSpec((1, D_MODEL), const), pl.BlockSpec((1, D_MODEL), const)])
    return pl.pallas_call(
        functools.partial(_mix_out_kernel, gated=gated, post_scale=post_scale),
        grid=(m // tm,),
        in_specs=in_specs,
        out_specs=[pl.BlockSpec((tm, D_MODEL), row)] * 2,
        out_shape=[jax.ShapeDtypeStruct((m, D_MODEL), F32), jax.ShapeDtypeStruct((m, D_MODEL), BF16)],
        compiler_params=_cparams(("parallel",)),
        name="mix_out",
    )(*args)


def _dense_ffn_kernel(xb_ref, h_ref, wg_ref, wu_ref, wd_ref, lg_ref, lb_ref, hf_ref, hb_ref, *, tf):
    x = xb_ref[...]
    f_dim = wg_ref.shape[1]
    acc = DEEPNORM_ALPHA * h_ref[...]
    for c in range(0, f_dim, tf):
        cw = min(tf, f_dim - c)
        g = _dot(x, wg_ref[:, c:c + cw])
        u = _dot(x, wu_ref[:, c:c + cw])
        acc = acc + _dot((_silu(g) * u).astype(BF16), wd_ref[c:c + cw, :])
    hn = _layer_norm(acc, lg_ref[...], lb_ref[...])
    hf_ref[...] = hn
    hb_ref[...] = hn.astype(BF16)


def _dense_ffn(xb, h, wg, wu, wd, ln_g, ln_b, *, tm=512, tf=512):
    m = xb.shape[0]
    f_dim = wg.shape[1]
    tm = _row_tile(m, tm)
    row = lambda i: (i, 0)
    const = lambda i: (0, 0)
    return pl.pallas_call(
        functools.partial(_dense_ffn_kernel, tf=tf),
        grid=(m // tm,),
        in_specs=[pl.BlockSpec((tm, D_MODEL), row), pl.BlockSpec((tm, D_MODEL), row),
                  _resident((D_MODEL, f_dim)), _resident((D_MODEL, f_dim)), _resident((f_dim, D_MODEL)),
                  pl.BlockSpec((1, D_MODEL), const), pl.BlockSpec((1, D_MODEL), const)],
        out_specs=[pl.BlockSpec((tm, D_MODEL), row)] * 2,
        out_shape=[jax.ShapeDtypeStruct((m, D_MODEL), F32), jax.ShapeDtypeStruct((m, D_MODEL), BF16)],
        compiler_params=_cparams(("parallel",)),
        name="dense_ffn",
    )(xb, h, wg, wu, wd, ln_g.reshape(1, -1), ln_b.reshape(1, -1))


MOE_TM = 256
MOE_BM = 512
MOE_ALIGN = BF16_ROWS


def _router_kernel(h_ref, wr_ref, a_ref, r_ref, g_ref, cnt_ref):
    tm = h_ref.shape[0]
    logits = _dot_nt(wr_ref[...], h_ref[...], precision=HIGHEST)
    e_idx = lax.broadcasted_iota(jnp.int32, logits.shape, 0).astype(F32)
    m1 = jnp.max(logits, axis=0, keepdims=True)
    i1 = jnp.min(jnp.where(logits == m1, e_idx, float(N_EXPERTS)), axis=0, keepdims=True)
    first = e_idx == i1
    rest = jnp.where(first, -jnp.inf, logits)
    m2 = jnp.max(rest, axis=0, keepdims=True)
    i2 = jnp.min(jnp.where(rest == m2, e_idx, float(N_EXPERTS)), axis=0, keepdims=True)
    second = e_idx == i2
    ex = jnp.exp(m2 - m1)
    g1 = 1.0 / (1.0 + ex)
    g2 = ex / (1.0 + ex)
    routed = jnp.where(first | second, 1.0, 0.0)
    ji = lax.broadcasted_iota(jnp.int32, (tm, tm), 0)
    ii = lax.broadcasted_iota(jnp.int32, (tm, tm), 1)
    before = jnp.where(ji < ii, 1.0, 0.0).astype(BF16)
    a_ref[...] = routed
    r_ref[...] = _dot(routed.astype(BF16), before)
    g_ref[...] = jnp.where(first, g1, 0.0) + jnp.where(second, g2, 0.0)
    cnt_ref[0] = jnp.broadcast_to(jnp.sum(routed, axis=1, keepdims=True), (N_EXPERTS, LANES))


def _router(h, w_router_t, tm):
    m = h.shape[0]
    nb = m // tm
    col = lambda i: (0, i)
    return pl.pallas_call(
        _router_kernel,
        grid=(nb,),
        in_specs=[pl.BlockSpec((tm, D_MODEL), lambda i: (i, 0)),
                  pl.BlockSpec((N_EXPERTS, D_MODEL), lambda i: (0, 0))],
        out_specs=[pl.BlockSpec((N_EXPERTS, tm), col)] * 3
                  + [pl.BlockSpec((1, N_EXPERTS, LANES), lambda i: (i, 0, 0))],
        out_shape=[jax.ShapeDtypeStruct((N_EXPERTS, m), F32)] * 3
                  + [jax.ShapeDtypeStruct((nb, N_EXPERTS, LANES), F32)],
        compiler_params=_cparams(("parallel",)),
        name="moe_router",
    )(h, w_router_t)


def _dispatch_kernel(off_ref, xb_ref, a_ref, r_ref, rows_in_ref, rows_ref, slab_ref, sem_ref):
    del rows_in_ref
    b = pl.program_id(0)
    nb = pl.num_programs(0)
    slot = b % 2
    tm = xb_ref.shape[0]

    def slab_copy(which_slot, which_block, e):
        off = pl.multiple_of(off_ref[which_block * N_EXPERTS + e], MOE_ALIGN)
        return pltpu.make_async_copy(slab_ref.at[which_slot, e], rows_ref.at[pl.ds(off, tm)],
                                     sem_ref.at[which_slot, e])

    x = xb_ref[...]
    r_idx = lax.broadcasted_iota(jnp.int32, (tm, tm), 0).astype(F32)
    for e in range(N_EXPERTS):
        pick = jnp.where((r_idx == r_ref[e:e + 1, :]) & (a_ref[e:e + 1, :] > 0.0), 1.0, 0.0)
        slab_ref[slot, e] = _dot(pick.astype(BF16), x).astype(BF16)

    @pl.when(b > 0)
    def _previous_block_landed():
        for e in range(N_EXPERTS):
            slab_copy(1 - slot, b - 1, e).wait()

    for e in range(N_EXPERTS):
        slab_copy(slot, b, e).start()

    @pl.when(b == nb - 1)
    def _drain():
        for e in range(N_EXPERTS):
            slab_copy(slot, b, e).wait()


def _dispatch(xb, a_t, r_t, off, n_rows, tm):
    m = xb.shape[0]
    nb = m // tm
    rows_init = jnp.zeros((n_rows, D_MODEL), BF16)
    grid_spec = pltpu.PrefetchScalarGridSpec(
        num_scalar_prefetch=1,
        grid=(nb,),
        in_specs=[pl.BlockSpec((tm, D_MODEL), lambda b, off: (b, 0)),
                  pl.BlockSpec((N_EXPERTS, tm), lambda b, off: (0, b)),
                  pl.BlockSpec((N_EXPERTS, tm), lambda b, off: (0, b)),
                  pl.BlockSpec(memory_space=pl.ANY)],
        out_specs=pl.BlockSpec(memory_space=pl.ANY),
        scratch_shapes=[pltpu.VMEM((2, N_EXPERTS, tm, D_MODEL), BF16),
                        pltpu.SemaphoreType.DMA((2, N_EXPERTS))],
    )
    return pl.pallas_call(
        _dispatch_kernel,
        grid_spec=grid_spec,
        out_shape=jax.ShapeDtypeStruct((n_rows, D_MODEL), BF16),
        input_output_aliases={4: 0},
        compiler_params=_cparams(("arbitrary",)),
        name="moe_dispatch",
    )(off, xb, a_t, r_t, rows_init)


def _expert_ffn_kernel(be_ref, valid_ref, x_ref, wg_ref, wu_ref, wd_ref, y_ref, acc_ref):
    del be_ref
    g_idx = pl.program_id(0)
    f = pl.program_id(1)
    nf = pl.num_programs(1)
    is_valid = valid_ref[g_idx] > 0

    @pl.when(is_valid)
    def _compute():
        x = x_ref[...]
        g = _dot(x, wg_ref[0])
        u = _dot(x, wu_ref[0])
        part = _dot((_silu(g) * u).astype(BF16), wd_ref[0])

        @pl.when(f == 0)
        def _():
            acc_ref[...] = part

        @pl.when(f > 0)
        def _():
            acc_ref[...] += part

        @pl.when(f == nf - 1)
        def _():
            y_ref[...] = acc_ref[...].astype(y_ref.dtype)

    @pl.when(jnp.logical_not(is_valid) & (f == nf - 1))
    def _skip():
        y_ref[...] = jnp.zeros_like(y_ref)


def _expert_ffn(x_rows, blk_e, valid, wg, wu, wd, bm, tf):
    n_rows = x_rows.shape[0]
    f_dim = wg.shape[2]
    nf = f_dim // tf
    def f_blk(g, f, va):
        return jnp.where(va[g] > 0, f, nf - 1)

    grid_spec = pltpu.PrefetchScalarGridSpec(
        num_scalar_prefetch=2,
        grid=(n_rows // bm, nf),
        in_specs=[pl.BlockSpec((bm, D_MODEL), lambda g, f, be, va: (g, 0)),
                  pl.BlockSpec((1, D_MODEL, tf), lambda g, f, be, va: (be[g], 0, f_blk(g, f, va))),
                  pl.BlockSpec((1, D_MODEL, tf), lambda g, f, be, va: (be[g], 0, f_blk(g, f, va))),
                  pl.BlockSpec((1, tf, D_MODEL), lambda g, f, be, va: (be[g], f_blk(g, f, va), 0))],
        out_specs=pl.BlockSpec((bm, D_MODEL), lambda g, f, be, va: (g, 0)),
        scratch_shapes=[pltpu.VMEM((bm, D_MODEL), F32)],
    )
    return pl.pallas_call(
        _expert_ffn_kernel,
        grid_spec=grid_spec,
        out_shape=jax.ShapeDtypeStruct((n_rows, D_MODEL), BF16),
        compiler_params=_cparams(("parallel", "arbitrary")),
        name="moe_expert_ffn",
    )(blk_e, valid, x_rows, wg, wu, wd)


def _combine_kernel(off_ref, h_ref, a_ref, r_ref, g_ref, lg_ref, lb_ref, rows_ref, hf_ref, hb_ref,
                    slab_ref, sem_ref, acc_ref):
    b = pl.program_id(0)
    nb = pl.num_programs(0)
    slot = b % 2
    tm = h_ref.shape[0]

    def slab_copy(which_slot, which_block, e):
        off = pl.multiple_of(off_ref[which_block * N_EXPERTS + e], MOE_ALIGN)
        return pltpu.make_async_copy(rows_ref.at[pl.ds(off, tm)], slab_ref.at[which_slot, e],
                                     sem_ref.at[which_slot, e])

    @pl.when(b == 0)
    def _prime():
        for e in range(N_EXPERTS):
            slab_copy(0, 0, e).start()

    @pl.when(b + 1 < nb)
    def _prefetch():
        for e in range(N_EXPERTS):
            slab_copy(1 - slot, b + 1, e).start()

    acc_ref[...] = DEEPNORM_ALPHA * h_ref[...]
    r_idx = lax.broadcasted_iota(jnp.int32, (tm, tm), 1).astype(F32)
    for e in range(N_EXPERTS):
        pick = jnp.where((r_idx == r_ref[:, e:e + 1]) & (a_ref[:, e:e + 1] > 0.0), 1.0, 0.0)
        slab_copy(slot, b, e).wait()
        acc_ref[...] += g_ref[:, e:e + 1] * _dot(pick.astype(BF16), slab_ref[slot, e])
    hn = _layer_norm(acc_ref[...], lg_ref[...], lb_ref[...])
    hf_ref[...] = hn
    hb_ref[...] = hn.astype(BF16)


def _combine(h, a_c, r_c, g_c, ln_g, ln_b, y_rows, off, tm):
    m = h.shape[0]
    nb = m // tm
    blk = lambda b, off: (b, 0)
    const = lambda b, off: (0, 0)
    grid_spec = pltpu.PrefetchScalarGridSpec(
        num_scalar_prefetch=1,
        grid=(nb,),
        in_specs=[pl.BlockSpec((tm, D_MODEL), blk),
                  pl.BlockSpec((tm, N_EXPERTS), blk), pl.BlockSpec((tm, N_EXPERTS), blk),
                  pl.BlockSpec((tm, N_EXPERTS), blk),
                  pl.BlockSpec((1, D_MODEL), const), pl.BlockSpec((1, D_MODEL), const),
                  pl.BlockSpec(memory_space=pl.ANY)],
        out_specs=[pl.BlockSpec((tm, D_MODEL), blk)] * 2,
        scratch_shapes=[pltpu.VMEM((2, N_EXPERTS, tm, D_MODEL), BF16), pltpu.SemaphoreType.DMA((2, N_EXPERTS)),
                        pltpu.VMEM((tm, D_MODEL), F32)],
    )
    return pl.pallas_call(
        _combine_kernel,
        grid_spec=grid_spec,
        out_shape=[jax.ShapeDtypeStruct((m, D_MODEL), F32), jax.ShapeDtypeStruct((m, D_MODEL), BF16)],
        compiler_params=_cparams(("arbitrary",)),
        name="moe_combine",
    )(off, h, a_c, r_c, g_c, ln_g.reshape(1, -1), ln_b.reshape(1, -1), y_rows)


def _moe_ffn(h, hb, w_router, wg, wu, wd, ln_g, ln_b):
    m = h.shape[0]
    tm = _row_tile(m, MOE_TM)
    nb = m // tm
    bm = MOE_BM if m >= N_EXPERTS * MOE_BM else tm
    f_dim = wg.shape[2]
    tf = f_dim // 2 if (f_dim // 2) % 256 == 0 else f_dim
    a_t, r_t, g_t, cnts = _router(h, w_router.T, tm)

    cnt = cnts[:, :, 0].astype(jnp.int32)
    padded = (cnt + MOE_ALIGN - 1) // MOE_ALIGN * MOE_ALIGN
    base = jnp.cumsum(padded, axis=0) - padded
    used = jnp.sum(padded, axis=0)
    cap = (used + tm + bm - 1) // bm * bm
    region = jnp.cumsum(cap) - cap
    off = (region[None, :] + base).reshape(-1).astype(jnp.int32)
    max_rows = 2 * m + nb * N_EXPERTS * (MOE_ALIGN - 1) + N_EXPERTS * (tm + bm)
    n_rows = (max_rows + bm - 1) // bm * bm
    g_idx = jnp.arange(n_rows // bm, dtype=jnp.int32)
    region_blk = region // bm
    blk_e = (jnp.sum(g_idx[:, None] >= region_blk[None, :], axis=1) - 1).astype(jnp.int32)
    n_blk = (used + bm - 1) // bm
    valid = ((g_idx - region_blk[blk_e]) < n_blk[blk_e]).astype(jnp.int32)

    x_rows = _dispatch(hb, a_t, r_t, off, n_rows, tm)
    y_rows = _expert_ffn(x_rows, blk_e, valid, wg, wu, wd, bm, tf)
    return _combine(h, a_t.T, r_t.T, g_t.T, ln_g, ln_b, y_rows, off, tm)


def _lambda_value(lam_ref, lam_init):
    s1 = jnp.sum(lam_ref[0:1, :] * lam_ref[1:2, :], axis=1, keepdims=True)
    s2 = jnp.sum(lam_ref[2:3, :] * lam_ref[3:4, :], axis=1, keepdims=True)
    return jnp.exp(s1) - jnp.exp(s2) + lam_init


def _attn_prompt_kernel(qi_ref, kj_ref, lam_ref, q_ref, k_ref, v_ref, o_ref, qq_ref, m_ref, acc_ref,
                        *, lam_init):
    p = pl.program_id(2)
    i = qi_ref[p]
    j = kj_ref[p]
    tq = q_ref.shape[0]
    tk = k_ref.shape[0]

    @pl.when(j == 0)
    def _init():
        q = q_ref[...]
        lane = lax.broadcasted_iota(jnp.int32, q.shape, 1)
        zero = jnp.zeros_like(q)
        qq_ref[0:tq, :] = jnp.where(lane < D_HEAD_B, q, zero)
        qq_ref[tq:2 * tq, :] = jnp.where(lane >= D_HEAD_B, q, zero)
        m_ref[...] = jnp.full_like(m_ref, -jnp.inf)
        acc_ref[...] = jnp.zeros_like(acc_ref)

    def step(masked):
        s = _dot_nt(qq_ref[...], k_ref[...])
        if masked:
            qpos = lax.broadcasted_iota(jnp.int32, s.shape, 0) % tq
            kpos = lax.broadcasted_iota(jnp.int32, s.shape, 1)
            s = jnp.where(kpos <= qpos, s, -jnp.inf)
        m_prev = m_ref[...]
        m_new = jnp.maximum(m_prev, jnp.max(s, axis=1, keepdims=True))
        alpha = jnp.exp2(m_prev - m_new)
        pr = jnp.exp2(s - jnp.concatenate([m_new] * (tk // LANES), axis=1))
        acc_ref[:, HEAD_DIM:] = alpha * acc_ref[:, HEAD_DIM:] + jnp.sum(pr, axis=1, keepdims=True)
        acc_ref[:, :HEAD_DIM] = alpha * acc_ref[:, :HEAD_DIM] + _dot(pr.astype(BF16), v_ref[...])
        m_ref[...] = m_new

    @pl.when(j < i)
    def _full():
        step(False)

    @pl.when(j == i)
    def _diag():
        step(True)
        lam = _lambda_value(lam_ref, lam_init)
        o = acc_ref[:, :HEAD_DIM] / acc_ref[:, HEAD_DIM:]
        o_ref[...] = o[:tq] - lam * o[tq:]


def _attn_prompt(q, k, v, lam_p, lam_init, *, n_seq, seq_len, tq=512):
    m = q.shape[0]
    tq = _row_tile(seq_len, tq)
    nq = seq_len // tq
    pairs = [(i, j) for i in range(nq) for j in range(i + 1)]
    qi = jnp.asarray([p[0] for p in pairs], jnp.int32)
    kj = jnp.asarray([p[1] for p in pairs], jnp.int32)
    lam_pad = jnp.zeros((SUBLANES, LANES), F32).at[:4, :D_HEAD_B].set(lam_p)
    grid_spec = pltpu.PrefetchScalarGridSpec(
        num_scalar_prefetch=2,
        grid=(n_seq, N_HEADS, len(pairs)),
        in_specs=[pl.BlockSpec((SUBLANES, LANES), lambda b, h, p, qi, kj: (0, 0)),
                  pl.BlockSpec((tq, HEAD_DIM), lambda b, h, p, qi, kj: (b * nq + qi[p], h)),
                  pl.BlockSpec((tq, HEAD_DIM), lambda b, h, p, qi, kj: (b * nq + kj[p], h)),
                  pl.BlockSpec((tq, HEAD_DIM), lambda b, h, p, qi, kj: (b * nq + kj[p], h))],
        out_specs=pl.BlockSpec((tq, HEAD_DIM), lambda b, h, p, qi, kj: (b * nq + qi[p], h)),
        scratch_shapes=[pltpu.VMEM((2 * tq, HEAD_DIM), BF16), pltpu.VMEM((2 * tq, LANES), F32),
                        pltpu.VMEM((2 * tq, HEAD_DIM + LANES), F32)],
    )
    assert tq % LANES == 0
    return pl.pallas_call(
        functools.partial(_attn_prompt_kernel, lam_init=lam_init),
        grid_spec=grid_spec,
        out_shape=jax.ShapeDtypeStruct((m, D_MODEL), F32),
        compiler_params=_cparams(("parallel", "parallel", "arbitrary")),
        name="attn_prompt",
    )(qi, kj, lam_pad, q, k, v)


SAMPLE_PAGE_GROUPS = 2


def _attn_sample_kernel(*refs, pages_per_step, t_real, lam_init):
    pp = pages_per_step
    pt_ref, lam_ref, q_ref, kn_ref, vn_ref = refs[:5]
    k_refs = refs[5:5 + pp]
    v_refs = refs[5 + pp:5 + 2 * pp]
    o_ref, qbd_ref, spread_ref, m_ref, l_ref, acc_ref = refs[5 + 2 * pp:]
    del pt_ref
    j = pl.program_id(1)
    n_rows = qbd_ref.shape[0]
    page = k_refs[0].shape[1]
    rows_per_head = 2 * SAMPLE_PAD_T

    def online_update(s_list, pv_fn):
        m_prev = m_ref[...]
        m_cur = jnp.max(s_list[0], axis=1, keepdims=True)
        for s in s_list[1:]:
            m_cur = jnp.maximum(m_cur, jnp.max(s, axis=1, keepdims=True))
        m_new = jnp.maximum(m_prev, m_cur)
        alpha = jnp.exp2(m_prev - m_new)
        m_wide = jnp.concatenate([m_new] * (page // LANES), axis=1)
        prs = [jnp.exp2(s - m_wide) for s in s_list]
        l_add = jnp.sum(prs[0], axis=1, keepdims=True)
        for pr in prs[1:]:
            l_add = l_add + jnp.sum(pr, axis=1, keepdims=True)
        l_ref[...] = alpha * l_ref[...] + l_add
        acc_ref[...] = alpha * acc_ref[...] + pv_fn([pr.astype(BF16) for pr in prs])
        m_ref[...] = m_new

    @pl.when(j == 0)
    def _init():
        rep = jnp.concatenate([q_ref[...]] * (n_rows // SAMPLE_PAD_T), axis=0)
        r_grp = lax.broadcasted_iota(jnp.int32, rep.shape, 0) // SAMPLE_PAD_T
        l_grp = lax.broadcasted_iota(jnp.int32, rep.shape, 1) // D_HEAD_B
        qbd_ref[...] = jnp.where(r_grp == l_grp, rep, 0.0).astype(BF16)
        key = lax.broadcasted_iota(jnp.int32, spread_ref.shape, 0)
        col = lax.broadcasted_iota(jnp.int32, spread_ref.shape, 1)
        spread_ref[...] = jnp.where(col // N_HEADS == key, 1.0, 0.0).astype(BF16)
        m_ref[...] = jnp.full_like(m_ref, -jnp.inf)
        l_ref[...] = jnp.zeros_like(l_ref)
        acc_ref[...] = jnp.zeros_like(acc_ref)
        pad = jnp.zeros((page - SAMPLE_PAD_T, D_MODEL), F32)
        k_pad = jnp.concatenate([kn_ref[...], pad], axis=0).astype(BF16)
        v_pad = jnp.concatenate([vn_ref[...], pad], axis=0).astype(BF16)
        s = _dot_nt(qbd_ref[...], k_pad)
        t_q = lax.broadcasted_iota(jnp.int32, s.shape, 0) % SAMPLE_PAD_T
        t_k = lax.broadcasted_iota(jnp.int32, s.shape, 1)
        s = jnp.where((t_k <= t_q) & (t_k < t_real), s, -jnp.inf)

        def pv_new(prs):
            return jnp.concatenate(
                [_dot(prs[0][h * rows_per_head:(h + 1) * rows_per_head],
                      v_pad[:, h * HEAD_DIM:(h + 1) * HEAD_DIM]) for h in range(N_HEADS)], axis=0)

        online_update([s], pv_new)

    def softmax_stage(scores):
        m_g = jnp.max(scores[0], axis=1, keepdims=True)
        for s in scores[1:]:
            m_g = jnp.maximum(m_g, jnp.max(s, axis=1, keepdims=True))
        prs = [jnp.exp2(s - m_g) for s in scores]
        l_g = jnp.sum(prs[0], axis=1, keepdims=True)
        for pr in prs[1:]:
            l_g = l_g + jnp.sum(pr, axis=1, keepdims=True)
        return m_g, l_g, [pr.astype(BF16) for pr in prs]

    def spread_stage(prs):
        row_head = lax.broadcasted_iota(jnp.int32, (n_rows, page * N_HEADS), 0) // rows_per_head
        col_head = lax.broadcasted_iota(jnp.int32, (n_rows, page * N_HEADS), 1) % N_HEADS
        own_head = row_head == col_head
        return [jnp.where(own_head, _dot(pr, spread_ref[...]), 0.0).astype(BF16) for pr in prs]

    def pv_stage(pages, wides):
        total = None
        for p_i, wide in zip(pages, wides):
            part = _dot(wide, v_refs[p_i][0].astype(BF16))
            total = part if total is None else total + part
        return total

    def merge(streams):
        m_prev = m_ref[...]
        m_new = m_prev
        for m_g, _, _ in streams:
            m_new = jnp.maximum(m_new, m_g)
        scale = jnp.exp2(m_prev - m_new)
        l_new = scale * l_ref[...]
        acc_new = scale * acc_ref[...]
        for m_g, l_g, acc_g in streams:
            w = jnp.exp2(m_g - m_new)
            l_new = l_new + w * l_g
            acc_new = acc_new + w * acc_g
        m_ref[...] = m_new
        l_ref[...] = l_new
        acc_ref[...] = acc_new

    def page_scores(kb):
        rows_per_tile = n_rows // (D_MODEL // MXU_DEPTH)
        return jnp.concatenate(
            [_dot_nt(qbd_ref[t * rows_per_tile:(t + 1) * rows_per_tile, t * MXU_DEPTH:(t + 1) * MXU_DEPTH],
                     kb[:, t * MXU_DEPTH:(t + 1) * MXU_DEPTH]) for t in range(D_MODEL // MXU_DEPTH)], axis=0)

    n_groups = min(SAMPLE_PAGE_GROUPS, pp)
    groups = [list(range(g_i * pp // n_groups, (g_i + 1) * pp // n_groups)) for g_i in range(n_groups)]
    scores = [[page_scores(k_refs[p_i][0].astype(BF16)) for p_i in grp] for grp in groups]
    soft, spreads = [], []
    for g_i in range(len(groups)):
        soft.append(softmax_stage(scores[g_i]))
        spreads.append(spread_stage(soft[g_i][2]))
    merge([(soft[g_i][0], soft[g_i][1], pv_stage(groups[g_i], spreads[g_i])) for g_i in range(len(groups))])

    @pl.when(j == pl.num_programs(1) - 1)
    def _finish():
        lam = _lambda_value(lam_ref, lam_init)
        o = acc_ref[...] / l_ref[...]
        for h in range(N_HEADS):
            r1 = h * rows_per_head
            r2 = r1 + SAMPLE_PAD_T
            o_ref[:, h * HEAD_DIM:(h + 1) * HEAD_DIM] = o[r1:r1 + SAMPLE_PAD_T] - lam * o[r2:r2 + SAMPLE_PAD_T]


def _attn_sample(q, k_new, v_new, cache_k, cache_v, page_table, lam_p, lam_init, *, t_real, pages_per_step=16):
    m = q.shape[0]
    n_seq, n_pages = page_table.shape
    page = cache_k.shape[1]
    pp = pages_per_step
    while n_pages % pp:
        pp //= 2
    lam_pad = jnp.zeros((SUBLANES, LANES), F32).at[:4, :D_HEAD_B].set(lam_p)
    n_rows = 2 * N_HEADS * SAMPLE_PAD_T
    seq = lambda s, j, pt: (s, 0)

    def page_spec(p_i, rows, width):
        return pl.BlockSpec((1, rows, width), lambda s, j, pt: (pt[s * n_pages + j * pp + p_i], 0, 0))

    grid_spec = pltpu.PrefetchScalarGridSpec(
        num_scalar_prefetch=1,
        grid=(n_seq, n_pages // pp),
        in_specs=[pl.BlockSpec((SUBLANES, LANES), lambda s, j, pt: (0, 0)),
                  pl.BlockSpec((SAMPLE_PAD_T, D_MODEL), seq), pl.BlockSpec((SAMPLE_PAD_T, D_MODEL), seq),
                  pl.BlockSpec((SAMPLE_PAD_T, D_MODEL), seq)]
                 + [page_spec(p_i, page, D_MODEL) for p_i in range(pp)]
                 + [page_spec(p_i, page * N_HEADS, HEAD_DIM) for p_i in range(pp)],
        out_specs=pl.BlockSpec((SAMPLE_PAD_T, D_MODEL), seq),
        scratch_shapes=[pltpu.VMEM((n_rows, D_MODEL), BF16), pltpu.VMEM((page, page * N_HEADS), BF16),
                        pltpu.VMEM((n_rows, LANES), F32), pltpu.VMEM((n_rows, LANES), F32),
                        pltpu.VMEM((n_rows, HEAD_DIM), F32)],
    )
    assert page == LANES and n_rows == LANES
    return pl.pallas_call(
        functools.partial(_attn_sample_kernel, pages_per_step=pp, t_real=t_real, lam_init=lam_init),
        grid_spec=grid_spec,
        out_shape=jax.ShapeDtypeStruct((m, D_MODEL), F32),
        compiler_params=_cparams(("parallel", "arbitrary")),
        name="attn_sample",
    )(page_table.reshape(-1), lam_pad, q, k_new, v_new, *([cache_k] * pp), *([cache_v] * pp))


def _rope_tables(pos):
    half = D_HEAD_B // 2
    inv_freq = ROPE_THETA ** (-jnp.arange(half, dtype=F32) / half)
    ang = pos.astype(F32)[:, None] * inv_freq[None, :]
    cos, sin = jnp.cos(ang), jnp.sin(ang)
    cos128 = jnp.tile(cos, (1, LANES // half))
    sin128 = jnp.tile(jnp.concatenate([-sin, sin], axis=1), (1, LANES // D_HEAD_B))
    return cos128, sin128


def _trunk(x, pos, conv_state, delta_state, past, wts, *, n_seq, seq_len, t_real):
    h = x
    hb = x.astype(BF16)
    cos128, sin128 = _rope_tables(pos)
    new_convs = []
    new_states = jnp.zeros((DEPTH // 2, n_seq, N_HEADS, HEAD_DIM, HEAD_DIM), F32)
    kf = vf = kb = vb = None
    for l in range(DEPTH):
        if l < DEPTH // 2:
            qkv_pre, z, ab = _linear(hb, wts["w_in"][l], (QKV_DIM, D_MODEL, LANES), (F32, BF16, F32))
            pre3 = qkv_pre.reshape(n_seq, seq_len, QKV_DIM)
            new_convs.append(pre3[:, t_real - (CONV_W - 1):t_real])
            if conv_state is None:
                prev = None
            else:
                prev = jnp.pad(conv_state[l], ((0, 0), (SAMPLE_PAD_T - (CONV_W - 1), 0), (0, 0)))
                prev = prev.reshape(n_seq * SAMPLE_PAD_T, QKV_DIM)
            q, k, v, gb = _conv_gate(qkv_pre, prev, ab, wts["w_conv"][l], wts["a_log"][l], wts["dt_bias"][l],
                                     seq_len=seq_len, t_real=t_real, chunk=DELTA_CHUNK)
            o, new_states = _delta_rule(q, k, v, gb, delta_state, new_states, l, DEPTH // 2,
                                        n_seq=n_seq, seq_len=seq_len, chunk=DELTA_CHUNK)
            h, hb = _mix_out(o, z, h, wts["g_norm"][l], wts["w_out"][l], wts["ln_mix_g"][l], wts["ln_mix_b"][l])
        else:
            j = l - DEPTH // 2
            lam_init = 0.8 - 0.6 * math.exp(-0.3 * l)
            if past is None:
                q = _q_proj(hb, wts["w_q"][j], cos128, sin128, BF16)
                o = _attn_prompt(q, kb, vb, wts["lambda"][j], lam_init, n_seq=n_seq, seq_len=seq_len)
            else:
                q = _q_proj(hb, wts["w_q"][j], cos128, sin128, F32)
                o = _attn_sample(q, kf, vf, past[0], past[1], past[2], wts["lambda"][j], lam_init, t_real=t_real)
            h, hb = _mix_out(o, None, h, wts["g_sub"][j], wts["w_o"][j], wts["ln_mix_g"][l], wts["ln_mix_b"][l],
                             post_scale=1.0 - lam_init)
        if l % 2 == 0:
            i = l // 2
            h, hb = _dense_ffn(hb, h, wts["w_gate_d"][i], wts["w_up_d"][i], wts["w_down_d"][i],
                               wts["ln_ffn_g"][l], wts["ln_ffn_b"][l])
        else:
            i = l // 2
            h, hb = _moe_ffn(h, hb, wts["w_router"][i], wts["w_gate_e"][i], wts["w_up_e"][i], wts["w_down_e"][i],
                             wts["ln_ffn_g"][l], wts["ln_ffn_b"][l])
        if l == DEPTH // 2 - 1:
            kf, vf, kb, vb = _kv_proj(hb, wts["w_kv"], cos128, sin128)
    return h, jnp.stack(new_convs), new_states, kf, vf


def kernel(x_prompt, x_sample, state_delta, state_conv, cache_k, cache_v, page_table, w_in_a, w_conv_a, a_log_a, dt_bias_a, g_norm_a, w_out_a, w_kv, w_q_b, lambda_b, g_sub_b, w_o_b, ln_mix_g, ln_mix_b, ln_ffn_g, ln_ffn_b, w_gate_d, w_up_d, w_down_d, w_router, w_gate_e, w_up_e, w_down_e):
    n_a = w_in_a.shape[0]
    bp, tp, _ = x_prompt.shape
    bs, ts, _ = x_sample.shape
    assert CONV_W - 1 <= ts <= SAMPLE_PAD_T and tp % SUBLANES == 0

    w_in = jnp.pad(w_in_a, ((0, 0), (0, 0), (0, LANES - 2 * N_HEADS))).astype(BF16)
    wts = dict(
        w_in=w_in, w_conv=w_conv_a, a_log=a_log_a, dt_bias=dt_bias_a, g_norm=g_norm_a,
        w_out=w_out_a.astype(BF16), w_kv=w_kv.astype(BF16), w_q=w_q_b.astype(BF16), g_sub=g_sub_b,
        w_o=w_o_b.astype(BF16), ln_mix_g=ln_mix_g, ln_mix_b=ln_mix_b, ln_ffn_g=ln_ffn_g, ln_ffn_b=ln_ffn_b,
        w_gate_d=w_gate_d.astype(BF16), w_up_d=w_up_d.astype(BF16), w_down_d=w_down_d.astype(BF16),
        w_router=w_router, w_gate_e=w_gate_e.astype(BF16), w_up_e=w_up_e.astype(BF16),
        w_down_e=w_down_e.astype(BF16))
    wts["lambda"] = lambda_b

    pos_p = jnp.tile(jnp.arange(tp), bp)
    y_p, conv_p, delta_p, k_p, v_p = _trunk(x_prompt.reshape(bp * tp, D_MODEL), pos_p, None, None, None, wts,
                                            n_seq=bp, seq_len=tp, t_real=tp)

    n_pages = page_table.shape[1]
    page = cache_k.shape[1]
    past_len = n_pages * page
    xs = jnp.pad(x_sample, ((0, 0), (0, SAMPLE_PAD_T - ts), (0, 0))).reshape(bs * SAMPLE_PAD_T, D_MODEL)
    pos_s = jnp.tile(past_len + jnp.arange(SAMPLE_PAD_T), bs)
    past = (cache_k.reshape(cache_k.shape[0], page, D_MODEL),
            cache_v.reshape(cache_v.shape[0], page * N_HEADS, HEAD_DIM), page_table)
    y_s, conv_s, delta_s, k_s, v_s = _trunk(xs, pos_s, state_conv, state_delta, past, wts,
                                            n_seq=bs, seq_len=SAMPLE_PAD_T, t_real=ts)

    def unpad(a):
        return a.reshape(bs, SAMPLE_PAD_T, D_MODEL)[:, :ts]

    return (y_p.reshape(bp, tp, D_MODEL), unpad(y_s), delta_p, conv_p,
            k_p.reshape(bp, tp, N_HEADS, 2, D_HEAD_B), v_p.reshape(bp, tp, N_HEADS, HEAD_DIM),
            delta_s, conv_s,
            unpad(k_s).reshape(bs, ts, N_HEADS, 2, D_HEAD_B), unpad(v_s).reshape(bs, ts, N_HEADS, HEAD_DIM))
```
